```python
import jax
import jax.numpy as jnp
from jax import lax
import numpy as np

D_MODEL = 2048
BATCH = 8
SEQ = 2048
DEPTH = 1

NSA_HEADS = 16
NSA_KV_GROUPS = 4
NSA_HPG = NSA_HEADS // NSA_KV_GROUPS
NSA_DH = 64
NSA_Q = NSA_HEADS * NSA_DH
NSA_KV = NSA_KV_GROUPS * NSA_DH
CMP_STRIDE = 16
CMP_LEN = 2 * CMP_STRIDE
CMP_HID = 256
SEL_LEN = 64
SEL_TOP = 16
WIN = 512
WIN_QBLK = 128
SEL_QBLK = 32
FORCE = 1e3
NEG = -1e9
M_HEADS = 4
M_DQK = 128
M_DV = 256
M_QK = M_HEADS * M_DQK
M_V = M_HEADS * M_DV
M_CHUNK = 64
CONV_W = 4
N_EXPERTS = 64
TOP_K = 8
N_GROUPS = 8
TOP_GROUPS = 4
EXPERT_FF = 512
SHARED_FF = 512
ROUTE_SCALE = 2.5
MOE_BLK = 128
EPS = 1e-6

IN_SIZES = (NSA_Q, 6 * NSA_KV, 3 * NSA_HEADS, 2 * M_QK, M_V, 2 * M_HEADS, M_V, D_MODEL, D_MODEL)
D_IN = sum(IN_SIZES)

kernel_name = 'hybrid_nsa_mlstm_moe_layer'


def rmsnorm(x, g):
    xf = x.astype(jnp.float32)
    y = xf * lax.rsqrt(jnp.mean(xf * xf, axis=-1, keepdims=True) + EPS)
    return (y * g.astype(jnp.float32)).astype(x.dtype)


def masked_softmax(s, mask):
    s = jnp.where(mask, s.astype(jnp.float32), NEG)
    return jax.nn.softmax(s, axis=-1) * mask


def swiglu(x, wg, wu, wd):
    return (jax.nn.silu(x @ wg) * (x @ wu)) @ wd


def causal_conv(x, w, b):
    T = x.shape[1]
    xp = jnp.pad(x, ((0, 0), (CONV_W - 1, 0), (0, 0)))
    y = b
    for i in range(CONV_W):
        y = y + xp[:, i:i + T] * w[i]
    return y


def compress_blocks(kv, pe, w1, w2):
    B, T, G, DH = kv.shape
    s = kv.reshape(B, T // CMP_STRIDE, CMP_STRIDE, G, DH)
    blocks = jnp.concatenate([s[:, :-1], s[:, 1:]], axis=2)
    blocks = blocks + pe[None, None, :, None, :]
    n_cmp = blocks.shape[1]
    flat = blocks.transpose(0, 1, 3, 2, 4).reshape(B, n_cmp, G, CMP_LEN * DH)
    return jax.nn.gelu(flat @ w1) @ w2


def nsa_mixer(q_a, kv_a, g_a, cmp_pe, cmp_w1, cmp_w2):
    B, T, _ = q_a.shape
    G, HPG, DH = NSA_KV_GROUPS, NSA_HPG, NSA_DH
    scale = DH ** -0.5
    q5 = q_a.reshape(B, T, G, HPG, DH)
    kv = kv_a.reshape(B, T, 6, G, DH)
    k_c, v_c, k_s, v_s, k_w, v_w = (kv[:, :, i] for i in range(6))
    t = jnp.arange(T)

    kc = compress_blocks(k_c, cmp_pe[0], cmp_w1[0], cmp_w2[0])
    vc = compress_blocks(v_c, cmp_pe[1], cmp_w1[1], cmp_w2[1])
    n_cmp = kc.shape[1]
    cmp_end = jnp.arange(n_cmp) * CMP_STRIDE + CMP_LEN - 1
    p_cmp = masked_softmax(jnp.einsum('btghd,bngd->bghtn', q5, kc) * scale,
                           cmp_end[None, :] <= t[:, None])
    o_cmp = jnp.einsum('bghtn,bngd->btghd', p_cmp.astype(vc.dtype), vc)

    n_sel = T // SEL_LEN
    top = min(SEL_TOP, n_sel)
    ci = jnp.arange(n_cmp)[:, None]
    sj = jnp.arange(n_sel)[None, :]
    overlap = (jnp.minimum(ci * CMP_STRIDE + CMP_LEN, (sj + 1) * SEL_LEN)
               - jnp.maximum(ci * CMP_STRIDE, sj * SEL_LEN))
    cmp_to_sel = jnp.clip(overlap, 0, None).astype(jnp.float32) / CMP_LEN
    imp = jnp.einsum('bgtn,nj->bgtj', p_cmp.sum(axis=2), cmp_to_sel)
    cur = (t // SEL_LEN)[:, None]
    forced = (sj == 0) | (sj == cur) | (sj == cur - 1)
    score = jnp.where(sj <= cur, imp + jnp.where(forced, FORCE, 0.0), NEG)
    top_s, top_idx = lax.top_k(score, top)
    top_ok = top_s > 0.5 * NEG
    ks_blk = k_s.reshape(B, n_sel, SEL_LEN, G, DH).transpose(0, 3, 1, 2, 4)
    vs_blk = v_s.reshape(B, n_sel, SEL_LEN, G, DH).transpose(0, 3, 1, 2, 4)
    nqs = T // SEL_QBLK
    bi = jnp.arange(B)[:, None, None, None]
    gi = jnp.arange(G)[None, :, None, None]

    def sel_block(args):
        qb, q_blk, idx, ok = args
        k_g = ks_blk[bi, gi, idx]
        v_g = vs_blk[bi, gi, idx].reshape(B, G, SEL_QBLK, top * SEL_LEN, DH)
        s = jnp.einsum('bqghd,bgqjld->bghqjl', q_blk, k_g) * scale
        tq = qb * SEL_QBLK + jnp.arange(SEL_QBLK)
        kpos = idx[..., None] * SEL_LEN + jnp.arange(SEL_LEN)
        mask = ok[..., None] & (kpos <= tq[None, None, :, None, None])
        p = masked_softmax(s.reshape(B, G, HPG, SEL_QBLK, top * SEL_LEN),
                           mask.reshape(B, G, 1, SEL_QBLK, top * SEL_LEN))
        return jnp.einsum('bghqn,bgqnd->bqghd', p.astype(v_g.dtype), v_g)

    o_slc = lax.map(sel_block, (jnp.arange(nqs),
                                jnp.moveaxis(q5.reshape(B, nqs, SEL_QBLK, G, HPG, DH), 1, 0),
                                jnp.moveaxis(top_idx.reshape(B, G, nqs, SEL_QBLK, top), 2, 0),
                                jnp.moveaxis(top_ok.reshape(B, G, nqs, SEL_QBLK, top), 2, 0)))
    o_slc = jnp.moveaxis(o_slc, 0, 1).reshape(B, T, G, HPG, DH)

    kw_pad = jnp.pad(k_w, ((0, 0), (WIN, 0), (0, 0), (0, 0)))
    vw_pad = jnp.pad(v_w, ((0, 0), (WIN, 0), (0, 0), (0, 0)))
    nqw = T // WIN_QBLK

    def win_block(args):
        qb, q_blk = args
        start = qb * WIN_QBLK
        k_b = lax.dynamic_slice_in_dim(kw_pad, start, WIN + WIN_QBLK, axis=1)
        v_b = lax.dynamic_slice_in_dim(vw_pad, start, WIN + WIN_QBLK, axis=1)
        s = jnp.einsum('bqghd,bsgd->bghqs', q_blk, k_b) * scale
        tq = start + jnp.arange(WIN_QBLK)
        kp = start - WIN + jnp.arange(WIN + WIN_QBLK)
        diff = tq[:, None] - kp[None, :]
        p = masked_softmax(s, (kp[None, :] >= 0) & (diff >= 0) & (diff < WIN))
        return jnp.einsum('bghqs,bsgd->bqghd', p.astype(v_b.dtype), v_b)

    o_win = lax.map(win_block, (jnp.arange(nqw),
                                jnp.moveaxis(q5.reshape(B, nqw, WIN_QBLK, G, HPG, DH), 1, 0)))
    o_win = jnp.moveaxis(o_win, 0, 1).reshape(B, T, G, HPG, DH)

    g = jax.nn.sigmoid(g_a).reshape(B, T, G, HPG, 3)
    o = g[..., 0:1] * o_cmp + g[..., 1:2] * o_slc + g[..., 2:3] * o_win
    return o.reshape(B, T, NSA_Q)


def mlstm_chunkwise(q, k, v, logi, logf):
    B, H, T, DQK = q.shape
    DV = v.shape[-1]
    L = M_CHUNK
    nc = T // L

    def chunk(z):
        return jnp.moveaxis(z.reshape((B, H, nc, L) + z.shape[3:]), 2, 0)

    tri = jnp.tril(jnp.ones((L, L), dtype=bool))

    def step(carry, xs):
        C, n, m = carry
        qc, kc, vc, li, lf = xs
        a = jnp.cumsum(lf, axis=-1)
        dlog = jnp.where(tri, a[..., :, None] - a[..., None, :] + li[..., None, :], -jnp.inf)
        inter = a + m[..., None]
        mt = jnp.maximum(inter, dlog.max(axis=-1))
        dmat = jnp.exp(dlog - mt[..., None])
        iw = jnp.exp(inter - mt)
        s = jnp.einsum('bhtd,bhsd->bhts', qc, kc) * dmat
        num = iw[..., None] * jnp.einsum('bhtd,bhde->bhte', qc, C) + jnp.einsum('bhts,bhse->bhte', s, vc)
        den = iw * jnp.einsum('bhtd,bhd->bht', qc, n) + s.sum(axis=-1)
        hc = num / jnp.maximum(jnp.abs(den), jnp.exp(-mt))[..., None]
        a_end = a[..., -1]
        elog = a_end[..., None] - a + li
        m_new = jnp.maximum(a_end + m, elog.max(axis=-1))
        ew = jnp.exp(elog - m_new[..., None])
        decay = jnp.exp(a_end + m - m_new)
        C_new = decay[..., None, None] * C + jnp.einsum('bhs,bhsd,bhse->bhde', ew, kc, vc)
        n_new = decay[..., None] * n + jnp.einsum('bhs,bhsd->bhd', ew, kc)
        return (C_new, n_new, m_new), hc

    init = (jnp.zeros((B, H, DQK, DV), jnp.float32), jnp.zeros((B, H, DQK), jnp.float32),
            jnp.zeros((B, H), jnp.float32))
    _, hs = lax.scan(step, init, (chunk(q), chunk(k), chunk(v), chunk(logi), chunk(logf)))
    return jnp.moveaxis(hs, 0, 2).reshape(B, H, T, DV)


def mlstm_mixer(qk_m, v_m, if_m, o_m, conv_w, conv_b, b_gates_m, mh_norm_g):
    B, T, _ = qk_m.shape
    dt = qk_m.dtype
    qk = jax.nn.silu(causal_conv(qk_m, conv_w, conv_b))
    q, k = jnp.split(qk, 2, axis=-1)

    def heads(z, d):
        return z.reshape(B, T, M_HEADS, d).transpose(0, 2, 1, 3).astype(jnp.float32)

    q = heads(q, M_DQK)
    k = heads(k, M_DQK) * (M_DQK ** -0.5)
    v = heads(v_m, M_DV)
    gates = (if_m + b_gates_m).astype(jnp.float32).transpose(0, 2, 1)
    logi = gates[:, :M_HEADS]
    logf = jax.nn.log_sigmoid(gates[:, M_HEADS:])
    hs = mlstm_chunkwise(q, k, v, logi, logf).transpose(0, 2, 1, 3)
    hn = rmsnorm(hs, mh_norm_g.reshape(M_HEADS, M_DV)).astype(dt)
    return (jax.nn.sigmoid(o_m).reshape(B, T, M_HEADS, M_DV) * hn).reshape(B, T, M_V)


def hybrid_mixer(h, w_in, cmp_pe, cmp_w1, cmp_w2, conv_w, conv_b, b_gates_m, mh_norm_g,
                 w_up_nsa, w_up_mlstm, w_out):
    proj = h @ w_in
    pts = np.cumsum(IN_SIZES)[:-1].tolist()
    q_a, kv_a, g_a, qk_m, v_m, if_m, o_m, gate_a, gate_b = jnp.split(proj, pts, axis=-1)
    o_nsa = nsa_mixer(q_a, kv_a, g_a, cmp_pe, cmp_w1, cmp_w2)
    o_mlstm = mlstm_mixer(qk_m, v_m, if_m, o_m, conv_w, conv_b, b_gates_m, mh_norm_g)
    merged = (jax.nn.sigmoid(gate_a) * (o_nsa @ w_up_nsa)
              + jax.nn.sigmoid(gate_b) * (o_mlstm @ w_up_mlstm))
    return merged @ w_out


def moe_ffn(h, w_router, b_router, w_e_gate, w_e_up, w_e_down, w_sh_gate, w_sh_up, w_sh_down):
    B, T, D = h.shape
    N = B * T
    hf = h.reshape(N, D)
    s = jax.nn.sigmoid((hf @ w_router).astype(jnp.float32))
    sb = s + b_router.astype(jnp.float32)
    gscore = lax.top_k(sb.reshape(N, N_GROUPS, N_EXPERTS // N_GROUPS), 2)[0].sum(axis=-1)
    _, gidx = lax.top_k(gscore, TOP_GROUPS)
    gmask = jax.nn.one_hot(gidx, N_GROUPS, dtype=jnp.float32).sum(axis=-2) > 0
    emask = jnp.repeat(gmask, N_EXPERTS // N_GROUPS, axis=-1)
    _, eidx = lax.top_k(jnp.where(emask, sb, NEG), TOP_K)
    w = jnp.take_along_axis(s, eidx, axis=-1)
    w = (w / w.sum(axis=-1, keepdims=True) * ROUTE_SCALE).astype(h.dtype)

    A = N * TOP_K
    e_flat = eidx.reshape(A)
    tok = jnp.repeat(jnp.arange(N, dtype=jnp.int32), TOP_K)
    w_flat = w.reshape(A)
    order = jnp.argsort(e_flat)
    e_s, tok_s, w_s = e_flat[order], tok[order], w_flat[order]
    counts = jnp.bincount(e_flat, length=N_EXPERTS)
    padded = (counts + MOE_BLK - 1) // MOE_BLK * MOE_BLK
    pad_end = jnp.cumsum(padded)
    pad_start = pad_end - padded
    raw_start = jnp.cumsum(counts) - counts
    dest = pad_start[e_s] + (jnp.arange(A) - raw_start[e_s])
    n_blk = -(-A // MOE_BLK) + N_EXPERTS
    P = n_blk * MOE_BLK
    slot_tok = jnp.full((P,), N, jnp.int32).at[dest].set(tok_s)
    slot_w = jnp.zeros((P,), h.dtype).at[dest].set(w_s)
    blk_exp = jnp.minimum(jnp.searchsorted(pad_end, jnp.arange(n_blk) * MOE_BLK, side='right'),
                          N_EXPERTS - 1)
    hp = jnp.concatenate([hf, jnp.zeros((1, D), hf.dtype)], axis=0)

    def run_block(args):
        e, toks, ws = args
        return swiglu(hp[toks], w_e_gate[e], w_e_up[e], w_e_down[e]) * ws[:, None]

    y = lax.map(run_block, (blk_exp, slot_tok.reshape(n_blk, MOE_BLK), slot_w.reshape(n_blk, MOE_BLK)))
    routed = jnp.zeros((N + 1, D), hf.dtype).at[slot_tok].add(y.reshape(P, D))[:N]
    shared = swiglu(hf, w_sh_gate, w_sh_up, w_sh_down)
    return (routed + shared).reshape(B, T, D)


def setup_inputs(seed: int = 0) -> dict:
    key = jax.random.key(seed)
    ks = iter(jax.random.split(key, 32))

    def nrm(shape, scale):
        return jax.random.normal(next(ks), shape, jnp.float32) * scale

    def gain(shape):
        return 1.0 + nrm(shape, 0.02)

    L = DEPTH
    return {
        'x': nrm((BATCH, SEQ, D_MODEL), 1.0),
        'c': nrm((BATCH, D_MODEL), 1.0),
        'w_ada': nrm((L, D_MODEL, 6 * D_MODEL), 0.2 * D_MODEL ** -0.5),
        'b_ada': nrm((L, 6 * D_MODEL), 0.01),
        'g_pre_mix': gain((L, D_MODEL)),
        'g_post_mix': gain((L, D_MODEL)),
        'w_in': nrm((L, D_MODEL, D_IN), D_MODEL ** -0.5),
        'cmp_pe': nrm((L, 2, CMP_LEN, NSA_DH), 0.02),
        'cmp_w1': nrm((L, 2, CMP_LEN * NSA_DH, CMP_HID), (CMP_LEN * NSA_DH) ** -0.5),
        'cmp_w2': nrm((L, 2, CMP_HID, NSA_DH), CMP_HID ** -0.5),
        'conv_w': nrm((L, CONV_W, 2 * M_QK), CONV_W ** -0.5),
        'conv_b': nrm((L, 2 * M_QK), 0.01),
        'b_gates_m': jnp.concatenate([nrm((L, M_HEADS), 0.1),
                                      jnp.linspace(3.0, 6.0, M_HEADS)[None, :] + nrm((L, M_HEADS), 0.1)],
                                     axis=-1),
        'mh_norm_g': gain((L, M_V)),
        'w_up_nsa': nrm((L, NSA_Q, D_MODEL), NSA_Q ** -0.5),
        'w_up_mlstm': nrm((L, M_V, D_MODEL), M_V ** -0.5),
        'w_out': nrm((L, D_MODEL, D_MODEL), D_MODEL ** -0.5),
        'g_pre_ffn': gain((L, D_MODEL)),
        'g_post_ffn': gain((L, D_MODEL)),
        'w_router': nrm((L, D_MODEL, N_EXPERTS), D_MODEL ** -0.5),
        'b_router': nrm((L, N_EXPERTS), 0.01),
        'w_e_gate': nrm((L, N_EXPERTS, D_MODEL, EXPERT_FF), D_MODEL ** -0.5),
        'w_e_up': nrm((L, N_EXPERTS, D_MODEL, EXPERT_FF), D_MODEL ** -0.5),
        'w_e_down': nrm((L, N_EXPERTS, EXPERT_FF, D_MODEL), EXPERT_FF ** -0.5),
        'w_sh_gate': nrm((L, D_MODEL, SHARED_FF), D_MODEL ** -0.5),
        'w_sh_up': nrm((L, D_MODEL, SHARED_FF), D_MODEL ** -0.5),
        'w_sh_down': nrm((L, SHARED_FF, D_MODEL), SHARED_FF ** -0.5),
    }


def reference(x, c, w_ada, b_ada, g_pre_mix, g_post_mix, w_in, cmp_pe, cmp_w1, cmp_w2, conv_w, conv_b,
              b_gates_m, mh_norm_g, w_up_nsa, w_up_mlstm, w_out, g_pre_ffn, g_post_ffn, w_router, b_router,
              w_e_gate, w_e_up, w_e_down, w_sh_gate, w_sh_up, w_sh_down):
    for l in range(DEPTH):
        mod = jax.nn.silu(c) @ w_ada[l] + b_ada[l]
        sh_m, sc_m, gt_m, sh_f, sc_f, gt_f = (m[:, None, :] for m in jnp.split(mod, 6, axis=-1))
        h = rmsnorm(x, g_pre_mix[l]) * (1.0 + sc_m) + sh_m
        y = hybrid_mixer(h, w_in[l], cmp_pe[l], cmp_w1[l], cmp_w2[l], conv_w[l], conv_b[l], b_gates_m[l],
                         mh_norm_g[l], w_up_nsa[l], w_up_mlstm[l], w_out[l])
        x = x + gt_m * rmsnorm(y, g_post_mix[l])
        h = rmsnorm(x, g_pre_ffn[l]) * (1.0 + sc_f) + sh_f
        y = moe_ffn(h, w_router[l], b_router[l], w_e_gate[l], w_e_up[l], w_e_down[l],
                    w_sh_gate[l], w_sh_up[l], w_sh_down[l])
        x = x + gt_f * rmsnorm(y, g_post_ffn[l])
    return x
```

```python
import functools

import numpy as np
import jax
import jax.numpy as jnp
from jax import lax
from jax.experimental import pallas as pl
from jax.experimental.pallas import tpu as pltpu

F32 = jnp.float32
BF16 = jnp.bfloat16
I32 = jnp.int32
U32 = jnp.uint32

D_MODEL = 2048
NSA_HEADS = 16
NSA_KV_GROUPS = 4
NSA_HPG = NSA_HEADS // NSA_KV_GROUPS
NSA_DH = 64
NSA_Q = NSA_HEADS * NSA_DH
NSA_KV = NSA_KV_GROUPS * NSA_DH
CMP_STRIDE = 16
CMP_LEN = 2 * CMP_STRIDE
CMP_HID = 256
SEL_LEN = 64
SEL_TOP = 16
WIN = 512
FORCE = 1e3
NEG = -1e9
M_HEADS = 4
M_DQK = 128
M_DV = 256
M_QK = M_HEADS * M_DQK
M_V = M_HEADS * M_DV
CONV_W = 4
N_EXPERTS = 64
TOP_K = 8
N_GROUPS = 8
GROUP_SIZE = N_EXPERTS // N_GROUPS
TOP_GROUPS = 4
EXPERT_FF = 512
SHARED_FF = 512
ROUTE_SCALE = 2.5
EPS = 1e-6

IN_SIZES = (NSA_Q, 6 * NSA_KV, 3 * NSA_HEADS, 2 * M_QK, M_V, 2 * M_HEADS, M_V, D_MODEL, D_MODEL)
D_IN = sum(IN_SIZES)

LANES = 128
VMEM_LIMIT = 56 * 1024 * 1024

OFF_Q = 0
OFF_KV = 1024
OFF_OM = 2560
OFF_GA = 3584
OFF_GATE_A = 4096
OFF_GATE_B = 6144
OFF_VM = 8192
OFF_QKM = 9216
PROJ_W = 10240

MLSTM_CHUNK = 256
ATT_TILE = 256
MOE_BLK = 256


def _cparams(*sem):
    return pltpu.CompilerParams(dimension_semantics=sem, vmem_limit_bytes=VMEM_LIMIT)


def _dot(a, b, **kw):
    return jnp.dot(a, b, preferred_element_type=F32, **kw)


def _dot_nt(a, b):
    return lax.dot_general(a, b, (((1,), (1,)), ((), ())), preferred_element_type=F32)


def _ada_kernel(c_ref, w_ref, b_ref, o_ref):
    c = c_ref[...]
    sc = (c * jax.nn.sigmoid(c)).astype(BF16)
    o_ref[...] = _dot(sc, w_ref[...].astype(BF16)) + b_ref[...]


def _ada(c, w_ada, b_ada):
    B, D = c.shape
    n_out = w_ada.shape[1]
    tn = 1024
    return pl.pallas_call(
        _ada_kernel,
        grid=(n_out // tn,),
        in_specs=[pl.BlockSpec((B, D), lambda j: (0, 0)),
                  pl.BlockSpec((D, tn), lambda j: (0, j)),
                  pl.BlockSpec((1, tn), lambda j: (0, j))],
        out_specs=pl.BlockSpec((B, tn), lambda j: (0, j)),
        out_shape=jax.ShapeDtypeStruct((B, n_out), F32),
        compiler_params=_cparams("arbitrary"),
        name="ada_mod",
    )(c, w_ada, b_ada.reshape(1, n_out))


def _norm_mod(x, g, shift, scale):
    ms = jnp.mean(x * x, axis=-1, keepdims=True)
    y = x * lax.rsqrt(ms + EPS) * g
    return y * (1.0 + scale) + shift


def _inproj_kernel(x_ref, mod_ref, g_ref, w_ref, wif_ref, o_ref, oif_ref, h_scr):
    @pl.when(pl.program_id(1) == 0)
    def _():
        h = _norm_mod(x_ref[...], g_ref[...], mod_ref[0, 0:1, :], mod_ref[0, 1:2, :])
        hb = h.astype(BF16)
        h_scr[...] = hb
        oif_ref[...] = _dot(hb, wif_ref[...])

    o_ref[...] = _dot(h_scr[...], w_ref[...]).astype(BF16)


def _inproj(x2, mod3, g, w_main, w_if, T):
    N, D = x2.shape
    tm, tn = 512, 1024
    per_b = T // tm
    return pl.pallas_call(
        _inproj_kernel,
        grid=(N // tm, PROJ_W // tn),
        in_specs=[pl.BlockSpec((tm, D), lambda i, j: (i, 0)),
                  pl.BlockSpec((1, 6, D), lambda i, j: (i // per_b, 0, 0)),
                  pl.BlockSpec((1, D), lambda i, j: (0, 0)),
                  pl.BlockSpec((D, tn), lambda i, j: (0, j)),
                  pl.BlockSpec((D, LANES), lambda i, j: (0, 0))],
        out_specs=[pl.BlockSpec((tm, tn), lambda i, j: (i, j)),
                   pl.BlockSpec((tm, LANES), lambda i, j: (i, 0))],
        out_shape=[jax.ShapeDtypeStruct((N, PROJ_W), BF16),
                   jax.ShapeDtypeStruct((N, LANES), F32)],
        scratch_shapes=[pltpu.VMEM((tm, D), BF16)],
        compiler_params=_cparams("arbitrary", "arbitrary"),
        name="in_proj",
    )(x2, mod3, g.reshape(1, D), w_main, w_if)


def _compress_kernel(sk_ref, sv_ref, pe_ref, w1_ref, w2_ref, o_ref):
    nb = sk_ref.shape[2]
    half = CMP_STRIDE * NSA_DH
    out = None
    for kv, s_ref in enumerate((sk_ref, sv_ref)):
        s = s_ref[0, 0].astype(F32)
        top = (s + pe_ref[kv, 0:1, :]).astype(BF16)
        bot = (s + pe_ref[kv, 1:2, :]).astype(BF16)
        a = _dot(top, w1_ref[kv, :half, :])
        b = _dot(bot, w1_ref[kv, half:, :])
        hid = jax.nn.gelu(a + pltpu.roll(b, nb - 1, axis=0))
        y = _dot(hid.astype(BF16), w2_ref[kv])
        out = y if out is None else out + y
    o_ref[0, 0] = out.astype(BF16)


def _compress(sk, sv, pe2, w1, w2x):
    B, G, nb, width = sk.shape
    blk = pl.BlockSpec((1, 1, nb, width), lambda b, g: (b, g, 0, 0))
    return pl.pallas_call(
        _compress_kernel,
        grid=(B, G),
        in_specs=[blk, blk,
                  pl.BlockSpec(pe2.shape, lambda b, g: (0, 0, 0)),
                  pl.BlockSpec(w1.shape, lambda b, g: (0, 0, 0)),
                  pl.BlockSpec(w2x.shape, lambda b, g: (0, 0, 0))],
        out_specs=pl.BlockSpec((1, 1, nb, LANES), lambda b, g: (b, g, 0, 0)),
        out_shape=jax.ShapeDtypeStruct((B, G, nb, LANES), BF16),
        compiler_params=_cparams("arbitrary", "arbitrary"),
        name="nsa_compress",
    )(sk, sv, pe2, w1, w2x)


def _nsa_kernel(q_ref, kvc_ref, kvs_ref, kvw_ref, ga_ref, c2s_ref, e_ref, o_ref,
                sbias_scr, wbias_scr, *, n_cmp, n_sel, top):
    qi = pl.program_id(2)
    tq = q_ref.shape[1]
    tk = tq
    n_kt = kvs_ref.shape[1] // tk
    scale = NSA_DH ** -0.5

    row = lax.broadcasted_iota(I32, (tq, LANES), 0)
    lane = lax.broadcasted_iota(I32, (tq, LANES), 1)
    t_abs = qi * tq + row
    lo = lane < NSA_DH

    row_t = lax.broadcasted_iota(I32, (tq, tk), 0)
    col_t = lax.broadcasted_iota(I32, (tq, tk), 1)
    for d in range(WIN // tk + 1):
        diff = d * tk + row_t - col_t
        wbias_scr[d] = jnp.where((diff >= 0) & (diff < WIN), 0.0, NEG)

    q2 = q_ref[0].astype(F32) * scale
    qe = []
    for pr in range(NSA_HPG // 2):
        qp = q2[:, pr * LANES:(pr + 1) * LANES]
        qe.append(jnp.where(lo, qp, 0.0).astype(BF16))
        qe.append(jnp.where(lo, pltpu.roll(qp, NSA_DH, axis=1), 0.0).astype(BF16))

    kvc = kvc_ref[0, 0]
    cmp_ok = (lane < n_cmp) & (lane * CMP_STRIDE + (CMP_LEN - 1) <= t_abs)
    psum = jnp.zeros((tq, LANES), F32)
    o_cmp = []
    for h in range(NSA_HPG):
        s = jnp.where(cmp_ok, _dot_nt(qe[h], kvc), NEG)
        m = jnp.max(s, axis=-1, keepdims=True)
        p = jnp.where(cmp_ok, jnp.exp(s - m), 0.0)
        l = jnp.sum(p, axis=-1, keepdims=True)
        p = p / jnp.where(l > 0.0, l, 1.0)
        psum = psum + p
        o_cmp.append(_dot(p.astype(BF16), kvc))

    imp = _dot(psum, c2s_ref[...], precision=lax.Precision.HIGHEST)
    cur = t_abs // SEL_LEN
    valid = (lane <= cur) & (lane < n_sel)
    forced = (lane == 0) | (lane == cur) | (lane == cur - 1)
    score = jnp.where(valid, imp + jnp.where(forced, FORCE, 0.0), NEG)
    rank = jnp.zeros((tq, LANES), I32)
    for j in range(n_sel):
        col = score[:, j:j + 1]
        beats = (col > score) | ((col == score) & (lane > j))
        rank = rank + beats.astype(I32)
    sel = (valid & (rank < top)).astype(BF16)

    for kt in range(n_kt):
        @pl.when(kt <= qi)
        def _():
            hit = _dot(sel, e_ref[:, kt * tk:(kt + 1) * tk])
            ok = (hit > 0.5) & (kt * tk + col_t <= qi * tq + row_t)
            sbias_scr[kt] = jnp.where(ok, 0.0, NEG)

    def flash(qh, kv_ref, bias_fn, kt_lo, kt_hi):
        def body(kt, carry):
            m, l, acc = carry
            off = pl.multiple_of(kt * tk, tk)
            kv = kv_ref[0, pl.ds(off, tk), :]
            s = _dot_nt(qh, kv) + bias_fn(kt)
            m_new = jnp.maximum(m, jnp.max(s, axis=-1, keepdims=True))
            alpha = jnp.exp(m - m_new)
            p = jnp.exp(s - m_new)
            l = alpha * l + jnp.sum(p, axis=-1, keepdims=True)
            acc = alpha * acc + _dot(p.astype(BF16), kv)
            return m_new, l, acc

        init = (jnp.full((tq, 1), -1e30, F32), jnp.zeros((tq, 1), F32), jnp.zeros((tq, LANES), F32))
        _, l, acc = lax.fori_loop(kt_lo, kt_hi, body, init)
        return acc / l

    gates = jax.nn.sigmoid(ga_ref[0].astype(F32))
    outs = []
    for h in range(NSA_HPG):
        o_slc = flash(qe[h], kvs_ref, lambda kt: sbias_scr[kt], 0, qi + 1)
        o_win = flash(qe[h], kvw_ref, lambda kt: wbias_scr[qi - kt],
                      jnp.maximum(qi - WIN // tk, 0), qi + 1)
        outs.append(gates[:, 3 * h:3 * h + 1] * o_cmp[h]
                    + gates[:, 3 * h + 1:3 * h + 2] * o_slc
                    + gates[:, 3 * h + 2:3 * h + 3] * o_win)
    for pr in range(NSA_HPG // 2):
        pair = jnp.where(lo, pltpu.roll(outs[2 * pr], NSA_DH, axis=1), outs[2 * pr + 1])
        o_ref[0, :, pr * LANES:(pr + 1) * LANES] = pair.astype(BF16)


def _nsa_attention(proj3, kvc, c2s, expand):
    B, T, _ = proj3.shape
    G = NSA_KV_GROUPS
    tq = min(ATT_TILE, T)
    nq = T // tq
    n_cmp = T // CMP_STRIDE - 1
    n_sel = T // SEL_LEN
    top = min(SEL_TOP, n_sel)
    gw = NSA_HPG * NSA_DH
    assert kvc.shape[2] <= LANES and n_sel <= LANES
    kvc = jnp.pad(kvc, ((0, 0), (0, 0), (0, LANES - kvc.shape[2]), (0, 0)))
    kern = functools.partial(_nsa_kernel, n_cmp=n_cmp, n_sel=n_sel, top=top)
    return pl.pallas_call(
        kern,
        grid=(B, G, nq),
        in_specs=[
            pl.BlockSpec((1, tq, gw), lambda b, g, i: (b, i, OFF_Q // gw + g)),
            pl.BlockSpec((1, 1, kvc.shape[2], LANES), lambda b, g, i: (b, g, 0, 0)),
            pl.BlockSpec((1, T, LANES), lambda b, g, i: (b, 0, OFF_KV // LANES + G + g)),
            pl.BlockSpec((1, T, LANES), lambda b, g, i: (b, 0, OFF_KV // LANES + 2 * G + g)),
            pl.BlockSpec((1, tq, LANES), lambda b, g, i: (b, i, OFF_GA // LANES + g)),
            pl.BlockSpec((LANES, LANES), lambda b, g, i: (0, 0)),
            pl.BlockSpec((LANES, T), lambda b, g, i: (0, 0)),
        ],
        out_specs=pl.BlockSpec((1, tq, gw), lambda b, g, i: (b, i, g)),
        out_shape=jax.ShapeDtypeStruct((B, T, NSA_Q), BF16),
        scratch_shapes=[pltpu.VMEM((nq, tq, tq), F32),
                        pltpu.VMEM((WIN // tq + 1, tq, tq), F32)],
        compiler_params=_cparams("arbitrary", "arbitrary", "arbitrary"),
        name="nsa_attention",
    )(proj3, kvc, proj3, proj3, proj3, c2s, expand)


def _log_sigmoid(x):
    return -(jnp.maximum(-x, 0.0) + jnp.log1p(jnp.exp(-jnp.abs(x))))


def _mlstm_kernel(q_ref, k_ref, v_ref, o_ref, gc_ref, gr_ref, cwq_ref, cwk_ref, cbq_ref, cbk_ref,
                  bg_ref, ng_ref, out_ref, q_scr, k_scr, c_scr):
    hd = pl.program_id(1)
    T = q_ref.shape[1]
    L = gr_ref.shape[3]
    nc = T // L
    dv = v_ref.shape[2]

    trow = lax.broadcasted_iota(I32, (T, M_DQK), 0)

    def conv_silu(x_ref, w_ref, b_ref):
        x = x_ref[0].astype(F32)
        y = b_ref[...] + w_ref[CONV_W - 1:CONV_W, :] * x
        for d in range(1, CONV_W):
            xs = jnp.where(trow >= d, pltpu.roll(x, d, axis=0), 0.0)
            y = y + w_ref[CONV_W - 1 - d:CONV_W - d, :] * xs
        return y * jax.nn.sigmoid(y)

    q_scr[...] = conv_silu(q_ref, cwq_ref, cbq_ref).astype(BF16)
    k_scr[...] = (conv_silu(k_ref, cwk_ref, cbk_ref) * (M_DQK ** -0.5)).astype(BF16)
    c_scr[...] = jnp.zeros_like(c_scr)

    lane_l = lax.broadcasted_iota(I32, (L, LANES), 1)
    sub8 = lax.broadcasted_iota(I32, (2 * M_HEADS, L), 0)
    r_i = lax.broadcasted_iota(I32, (L, L), 0)
    c_i = lax.broadcasted_iota(I32, (L, L), 1)
    tri = c_i <= r_i
    ones_blk = jnp.where(lane_l == 0, 1.0, 0.0).astype(BF16)

    bl = lax.broadcasted_iota(I32, (1, LANES), 1)
    bg = bg_ref[...]
    b_i = jnp.sum(jnp.where(bl == hd, bg, 0.0), axis=-1, keepdims=True)
    b_f = jnp.sum(jnp.where(bl == hd + M_HEADS, bg, 0.0), axis=-1, keepdims=True)

    def chunk(c, m):
        off = pl.multiple_of(c * L, L)
        gc = gc_ref[0, pl.ds(off, L), :]
        gr = gr_ref[0, c]
        li_col = jnp.sum(jnp.where(lane_l == hd, gc, 0.0), axis=-1, keepdims=True) + b_i
        lf_col = _log_sigmoid(jnp.sum(jnp.where(lane_l == hd + M_HEADS, gc, 0.0), axis=-1, keepdims=True) + b_f)
        li_row = jnp.sum(jnp.where(sub8 == hd, gr, 0.0), axis=0, keepdims=True) + b_i
        lf_row = _log_sigmoid(jnp.sum(jnp.where(sub8 == hd + M_HEADS, gr, 0.0), axis=0, keepdims=True) + b_f)
        a_col = jnp.sum(jnp.where(tri, lf_row, 0.0), axis=-1, keepdims=True)
        a_row = jnp.sum(jnp.where(r_i <= c_i, lf_col, 0.0), axis=0, keepdims=True)
        a_end = jnp.sum(lf_row, axis=-1, keepdims=True)

        dlog = jnp.where(tri, a_col - a_row + li_row, -jnp.inf)
        inter = a_col + m
        mt = jnp.maximum(inter, jnp.max(dlog, axis=-1, keepdims=True))
        dmat = jnp.exp(dlog - mt)
        iw = jnp.exp(inter - mt)

        qc = q_scr[pl.ds(off, L), :]
        kc = k_scr[pl.ds(off, L), :]
        v_aug = jnp.concatenate([v_ref[0, pl.ds(off, L), :], ones_blk], axis=-1)
        s = (_dot_nt(qc, kc) * dmat).astype(BF16)
        cmat = c_scr[...]
        num = iw * _dot(qc, cmat.astype(BF16)) + _dot(s, v_aug)
        den = num[:, dv:dv + 1]
        hc = num[:, :dv] / jnp.maximum(jnp.abs(den), jnp.exp(-mt))

        elog = a_end - a_col + li_col
        m_new = jnp.maximum(a_end + m, jnp.max(elog, axis=0, keepdims=True))
        ew = jnp.exp(elog - m_new)
        decay = jnp.exp(a_end + m - m_new)
        kct = jnp.transpose(kc.astype(F32)).astype(BF16)
        c_scr[...] = decay * cmat + _dot(kct, (ew * v_aug.astype(F32)).astype(BF16))

        hn = hc * lax.rsqrt(jnp.mean(hc * hc, axis=-1, keepdims=True) + EPS) * ng_ref[...]
        og = jax.nn.sigmoid(o_ref[0, pl.ds(off, L), :].astype(F32))
        out_ref[0, pl.ds(off, L), :] = (og * hn).astype(BF16)
        return m_new

    lax.fori_loop(0, nc, chunk, jnp.zeros((1, 1), F32))


def _mlstm(proj3, gates_col, gates_row, conv_w, conv_b, b_gates, norm_g):
    B, T, _ = proj3.shape
    L = gates_row.shape[3]
    nc = T // L
    H = M_HEADS
    qb = OFF_QKM // M_DQK
    return pl.pallas_call(
        _mlstm_kernel,
        grid=(B, H),
        in_specs=[
            pl.BlockSpec((1, T, M_DQK), lambda b, h: (b, 0, qb + h)),
            pl.BlockSpec((1, T, M_DQK), lambda b, h: (b, 0, qb + H + h)),
            pl.BlockSpec((1, T, M_DV), lambda b, h: (b, 0, OFF_VM // M_DV + h)),
            pl.BlockSpec((1, T, M_DV), lambda b, h: (b, 0, OFF_OM // M_DV + h)),
            pl.BlockSpec((1, T, LANES), lambda b, h: (b, 0, 0)),
            pl.BlockSpec((1, nc, 2 * H, L), lambda b, h: (b, 0, 0, 0)),
            pl.BlockSpec((CONV_W, M_DQK), lambda b, h: (0, h)),
            pl.BlockSpec((CONV_W, M_DQK), lambda b, h: (0, H + h)),
            pl.BlockSpec((1, M_DQK), lambda b, h: (0, h)),
            pl.BlockSpec((1, M_DQK), lambda b, h: (0, H + h)),
            pl.BlockSpec((1, LANES), lambda b, h: (0, 0)),
            pl.BlockSpec((1, M_DV), lambda b, h: (0, h)),
        ],
        out_specs=pl.BlockSpec((1, T, M_DV), lambda b, h: (b, 0, h)),
        out_shape=jax.ShapeDtypeStruct((B, T, M_V), BF16),
        scratch_shapes=[pltpu.VMEM((T, M_DQK), BF16), pltpu.VMEM((T, M_DQK), BF16),
                        pltpu.VMEM((M_DQK, M_DV + LANES), F32)],
        compiler_params=_cparams("arbitrary", "arbitrary"),
        name="mlstm",
    )(proj3, proj3, proj3, proj3, gates_col, gates_row, conv_w, conv_w,
      conv_b.reshape(1, -1), conv_b.reshape(1, -1),
      jnp.pad(b_gates, (0, LANES - 2 * H)).reshape(1, LANES), norm_g.reshape(1, -1))


def _merge_kernel(on_ref, om_ref, ga_ref, gb_ref, wn_ref, wm_ref, o_ref):
    a = _dot(on_ref[...], wn_ref[...])
    b = _dot(om_ref[...], wm_ref[...])
    ga = jax.nn.sigmoid(ga_ref[...].astype(F32))
    gb = jax.nn.sigmoid(gb_ref[...].astype(F32))
    o_ref[...] = (ga * a + gb * b).astype(BF16)


def _merge(o_nsa, o_ml, proj, w_up_nsa, w_up_ml):
    N = o_nsa.shape[0]
    D = D_MODEL
    tm = 512
    return pl.pallas_call(
        _merge_kernel,
        grid=(N // tm,),
        in_specs=[pl.BlockSpec((tm, NSA_Q), lambda i: (i, 0)),
                  pl.BlockSpec((tm, M_V), lambda i: (i, 0)),
                  pl.BlockSpec((tm, D), lambda i: (i, OFF_GATE_A // D)),
                  pl.BlockSpec((tm, D), lambda i: (i, OFF_GATE_B // D)),
                  pl.BlockSpec((NSA_Q, D), lambda i: (0, 0)),
                  pl.BlockSpec((M_V, D), lambda i: (0, 0))],
        out_specs=pl.BlockSpec((tm, D), lambda i: (i, 0)),
        out_shape=jax.ShapeDtypeStruct((N, D), BF16),
        compiler_params=_cparams("arbitrary"),
        name="mixer_merge",
    )(o_nsa, o_ml, proj, proj, w_up_nsa, w_up_ml)


def _outproj_kernel(m_ref, w_ref, x_ref, mod_ref, g_ref, o_ref):
    y = _dot(m_ref[...], w_ref[...])
    ms = jnp.mean(y * y, axis=-1, keepdims=True)
    o_ref[...] = x_ref[...] + mod_ref[0, 2:3, :] * (y * lax.rsqrt(ms + EPS) * g_ref[...])


def _outproj(merged, w_out, x2, mod3, g, T):
    N, D = x2.shape
    tm = 512
    per_b = T // tm
    return pl.pallas_call(
        _outproj_kernel,
        grid=(N // tm,),
        in_specs=[pl.BlockSpec((tm, D), lambda i: (i, 0)),
                  pl.BlockSpec((D, D), lambda i: (0, 0)),
                  pl.BlockSpec((tm, D), lambda i: (i, 0)),
                  pl.BlockSpec((1, 6, D), lambda i: (i // per_b, 0, 0)),
                  pl.BlockSpec((1, D), lambda i: (0, 0))],
        out_specs=pl.BlockSpec((tm, D), lambda i: (i, 0)),
        out_shape=jax.ShapeDtypeStruct((N, D), F32),
        compiler_params=_cparams("arbitrary"),
        name="out_proj",
    )(merged, w_out, x2, mod3, g.reshape(1, D))


def _pack_rows(y):
    half = y.shape[1] // 2
    bits = lax.bitcast_convert_type(y.astype(BF16).astype(F32), U32)
    return (bits[:, half:] & jnp.uint32(0xFFFF0000)) | (bits[:, :half] >> 16)


def _unpack_rows(w):
    lo = lax.bitcast_convert_type(w << 16, F32)
    hi = lax.bitcast_convert_type(w & jnp.uint32(0xFFFF0000), F32)
    return lo, hi


def _ffn_in_kernel(x_ref, mod_ref, g_ref, wr_ref, wgu_ref, wd_ref, hp_ref, lg_ref, sh_ref):
    h = _norm_mod(x_ref[...], g_ref[...], mod_ref[0, 3:4, :], mod_ref[0, 4:5, :])
    lg_ref[...] = _dot(h, wr_ref[...], precision=lax.Precision.HIGHEST)
    hb = h.astype(BF16)
    hp_ref[...] = _pack_rows(h)
    gu = _dot(hb, wgu_ref[...])
    ff = gu.shape[1] // 2
    gate, up = gu[:, :ff], gu[:, ff:]
    act = (gate * jax.nn.sigmoid(gate) * up).astype(BF16)
    sh_ref[...] = _dot(act, wd_ref[...]).astype(BF16)


def _ffn_in(x1, mod3, g, w_router_pad, w_sh_gu, w_sh_down, T):
    N, D = x1.shape
    tm = 512
    per_b = T // tm
    return pl.pallas_call(
        _ffn_in_kernel,
        grid=(N // tm,),
        in_specs=[pl.BlockSpec((tm, D), lambda i: (i, 0)),
                  pl.BlockSpec((1, 6, D), lambda i: (i // per_b, 0, 0)),
                  pl.BlockSpec((1, D), lambda i: (0, 0)),
                  pl.BlockSpec((D, LANES), lambda i: (0, 0)),
                  pl.BlockSpec(w_sh_gu.shape, lambda i: (0, 0)),
                  pl.BlockSpec(w_sh_down.shape, lambda i: (0, 0))],
        out_specs=[pl.BlockSpec((tm, D // 2), lambda i: (i, 0)),
                   pl.BlockSpec((tm, LANES), lambda i: (i, 0)),
                   pl.BlockSpec((tm, D), lambda i: (i, 0))],
        out_shape=[jax.ShapeDtypeStruct((N, D // 2), U32),
                   jax.ShapeDtypeStruct((N, LANES), F32),
                   jax.ShapeDtypeStruct((N, D), BF16)],
        compiler_params=_cparams("arbitrary"),
        name="ffn_in",
    )(x1, mod3, g.reshape(1, D), w_router_pad, w_sh_gu, w_sh_down)


def _route_kernel(lg_ref, b_ref, tri_ref, e_ref, r_ref, w_ref, cnt_ref, run_scr):
    tm = lg_ref.shape[0]
    E, GS, NG = N_EXPERTS, GROUP_SIZE, N_GROUPS

    @pl.when(pl.program_id(0) == 0)
    def _():
        run_scr[...] = jnp.zeros_like(run_scr)

    s = jax.nn.sigmoid(jnp.transpose(lg_ref[...])[:E, :])
    sb = s + b_ref[:, 0:1]
    gi = lax.broadcasted_iota(I32, (NG, tm), 0)
    gs = jnp.zeros((NG, tm), F32)
    for j in range(NG):
        blk = sb[j * GS:(j + 1) * GS, :]
        m1 = jnp.max(blk, axis=0, keepdims=True)
        first = jnp.min(jnp.where(blk == m1, gi, GS), axis=0, keepdims=True)
        m2 = jnp.max(jnp.where(gi == first, -jnp.inf, blk), axis=0, keepdims=True)
        gs = jnp.where(gi == j, m1 + m2, gs)

    grank = jnp.zeros((NG, tm), I32)
    for j in range(NG):
        rowv = gs[j:j + 1, :]
        grank = grank + ((rowv > gs) | ((rowv == gs) & (gi > j))).astype(I32)
    gsel = jnp.where(grank < TOP_GROUPS, 1.0, 0.0)

    ei = lax.broadcasted_iota(I32, (E, tm), 0)
    esel = jnp.zeros((E, tm), F32)
    for j in range(NG):
        esel = jnp.where(ei // GS == j, gsel[j:j + 1, :], esel)
    masked = jnp.where(esel > 0.5, sb, NEG)

    erank = jnp.zeros((E, tm), I32)
    for j in range(E):
        rowv = masked[j:j + 1, :]
        erank = erank + ((rowv > masked) | ((rowv == masked) & (ei > j))).astype(I32)
    selm = erank < TOP_K

    wsum = jnp.sum(jnp.where(selm, s, 0.0), axis=0, keepdims=True)
    wfull = jnp.where(selm, s / wsum * ROUTE_SCALE, 0.0)

    self32 = selm.astype(F32)
    before = _dot(self32.astype(BF16), tri_ref[...]) + run_scr[:, 0:1]
    run_new = run_scr[:, 0:1] + jnp.sum(self32, axis=1, keepdims=True)
    run_scr[...] = jnp.broadcast_to(run_new, run_scr.shape)
    cnt_ref[...] = jnp.broadcast_to(run_new, cnt_ref.shape).astype(I32)

    before_i = before.astype(I32)
    for k in range(TOP_K):
        hit = selm & (erank == k)
        e_ref[k:k + 1, :] = jnp.sum(jnp.where(hit, ei, 0), axis=0, keepdims=True)
        r_ref[k:k + 1, :] = jnp.sum(jnp.where(hit, before_i, 0), axis=0, keepdims=True)
        w_ref[k:k + 1, :] = jnp.sum(jnp.where(hit, wfull, 0.0), axis=0, keepdims=True)


def _route(logits, b_router):
    N = logits.shape[0]
    tm = 256
    tri = jnp.asarray(np.triu(np.ones((tm, tm), np.float32), 1), BF16)
    b_col = jnp.broadcast_to(b_router.reshape(N_EXPERTS, 1), (N_EXPERTS, LANES))
    kspec = pl.BlockSpec((TOP_K, tm), lambda i: (0, i))
    return pl.pallas_call(
        _route_kernel,
        grid=(N // tm,),
        in_specs=[pl.BlockSpec((tm, LANES), lambda i: (i, 0)),
                  pl.BlockSpec((N_EXPERTS, LANES), lambda i: (0, 0)),
                  pl.BlockSpec((tm, tm), lambda i: (0, 0))],
        out_specs=[kspec, kspec, kspec, pl.BlockSpec((N_EXPERTS, LANES), lambda i: (0, 0))],
        out_shape=[jax.ShapeDtypeStruct((TOP_K, N), I32), jax.ShapeDtypeStruct((TOP_K, N), I32),
                   jax.ShapeDtypeStruct((TOP_K, N), F32), jax.ShapeDtypeStruct((N_EXPERTS, LANES), I32)],
        scratch_shapes=[pltpu.VMEM((N_EXPERTS, LANES), F32)],
        compiler_params=_cparams("arbitrary"),
        name="moe_route",
    )(logits, b_col, tri)


def _dispatch_kernel(e_ref, r_ref, ps_ref, cnt_ref, pe_ref, h_ref, xs_ref, zero_scr, sem, zsem):
    i = pl.program_id(0)
    tm = h_ref.shape[0]

    def row_copy(src, dst_row, s):
        return pltpu.make_async_copy(src, xs_ref.at[pl.ds(dst_row, 1), :], s)

    @pl.when(i == 0)
    def _():
        zero_scr[...] = jnp.zeros_like(zero_scr)

        def per_expert(e, n):
            lo = ps_ref[e] + cnt_ref[e]
            hi = pe_ref[e]

            def issue(rw, c):
                row_copy(zero_scr.at[pl.ds(0, 1), :], rw, zsem).start()
                return c
            lax.fori_loop(lo, hi, issue, 0)
            return n + (hi - lo)
        n_pad = lax.fori_loop(0, N_EXPERTS, per_expert, 0)

        def drain(_, c):
            row_copy(zero_scr.at[pl.ds(0, 1), :], 0, zsem).wait()
            return c
        lax.fori_loop(0, n_pad, drain, 0)

    def issue_row(r, c):
        for k in range(TOP_K):
            dst = ps_ref[e_ref[k, r]] + r_ref[k, r]
            row_copy(h_ref.at[pl.ds(r, 1), :], dst, sem).start()
        return c
    lax.fori_loop(0, tm, issue_row, 0)

    def drain_row(_, c):
        row_copy(h_ref.at[pl.ds(0, 1), :], 0, sem).wait()
        return c
    lax.fori_loop(0, tm * TOP_K, drain_row, 0)


def _dispatch(eidx, rnk, pad_start, counts, pad_end, h_packed, n_rows):
    N, W = h_packed.shape
    tm = 256
    smem = lambda shape, imap: pl.BlockSpec(shape, imap, memory_space=pltpu.SMEM)
    whole = lambda: pl.BlockSpec(memory_space=pltpu.SMEM)
    return pl.pallas_call(
        _dispatch_kernel,
        grid=(N // tm,),
        in_specs=[smem((TOP_K, tm), lambda i: (0, i)), smem((TOP_K, tm), lambda i: (0, i)),
                  whole(), whole(), whole(),
                  pl.BlockSpec((tm, W), lambda i: (i, 0))],
        out_specs=pl.BlockSpec(memory_space=pl.ANY),
        out_shape=jax.ShapeDtypeStruct((n_rows, W), U32),
        scratch_shapes=[pltpu.VMEM((8, W), U32), pltpu.SemaphoreType.DMA, pltpu.SemaphoreType.DMA],
        compiler_params=_cparams("arbitrary"),
        name="moe_dispatch",
    )(eidx, rnk, pad_start, counts, pad_end, h_packed)


def _expert_kernel(bexp_ref, nused_ref, xs_ref, wg_ref, wu_ref, wd_ref, ys_ref):
    @pl.when(pl.program_id(0) < nused_ref[0])
    def _():
        lo, hi = _unpack_rows(xs_ref[...])
        lo, hi = lo.astype(BF16), hi.astype(BF16)
        half = lo.shape[1]
        gate = _dot(lo, wg_ref[0, :half, :]) + _dot(hi, wg_ref[0, half:, :])
        up = _dot(lo, wu_ref[0, :half, :]) + _dot(hi, wu_ref[0, half:, :])
        act = (gate * jax.nn.sigmoid(gate) * up).astype(BF16)
        ys_ref[...] = _pack_rows(_dot(act, wd_ref[0]))


def _experts(blk_exp, n_used, xs, w_gate, w_up, w_down):
    P, W = xs.shape
    n_blk = P // MOE_BLK
    D, FF = w_gate.shape[1], w_gate.shape[2]
    row_map = lambda i, be, nu: (jnp.minimum(i, nu[0] - 1), 0)
    return pl.pallas_call(
        _expert_kernel,
        grid_spec=pltpu.PrefetchScalarGridSpec(
            num_scalar_prefetch=2,
            grid=(n_blk,),
            in_specs=[pl.BlockSpec((MOE_BLK, W), row_map),
                      pl.BlockSpec((1, D, FF), lambda i, be, nu: (be[i], 0, 0)),
                      pl.BlockSpec((1, D, FF), lambda i, be, nu: (be[i], 0, 0)),
                      pl.BlockSpec((1, FF, D), lambda i, be, nu: (be[i], 0, 0))],
            out_specs=pl.BlockSpec((MOE_BLK, W), row_map)),
        out_shape=jax.ShapeDtypeStruct((P, W), U32),
        compiler_params=_cparams("arbitrary"),
        name="moe_experts",
    )(blk_exp, n_used, xs, w_gate, w_up, w_down)


def _combine_kernel(e_ref, r_ref, ps_ref, w_ref, sh_ref, x_ref, mod_ref, g_ref, ys_ref, o_ref, buf, sem):
    tm = x_ref.shape[0]

    def row_copy(src_row, k, r):
        return pltpu.make_async_copy(ys_ref.at[pl.ds(src_row, 1), :], buf.at[k, pl.ds(r, 1), :], sem)

    def issue_row(r, c):
        for k in range(TOP_K):
            row_copy(ps_ref[e_ref[k, r]] + r_ref[k, r], k, r).start()
        return c
    lax.fori_loop(0, tm, issue_row, 0)

    def drain_row(_, c):
        row_copy(0, 0, 0).wait()
        return c
    lax.fori_loop(0, tm * TOP_K, drain_row, 0)

    half = buf.shape[2]
    w = w_ref[...]
    acc_lo = sh_ref[:, :half].astype(F32)
    acc_hi = sh_ref[:, half:].astype(F32)
    for k in range(TOP_K):
        lo, hi = _unpack_rows(buf[k])
        wk = w[:, k:k + 1]
        acc_lo = acc_lo + wk * lo
        acc_hi = acc_hi + wk * hi
    ms = (jnp.sum(acc_lo * acc_lo, axis=-1, keepdims=True)
          + jnp.sum(acc_hi * acc_hi, axis=-1, keepdims=True)) / (2 * half)
    inv = lax.rsqrt(ms + EPS)
    o_ref[:, :half] = x_ref[:, :half] + mod_ref[0, 5:6, :half] * (acc_lo * inv * g_ref[:, :half])
    o_ref[:, half:] = x_ref[:, half:] + mod_ref[0, 5:6, half:] * (acc_hi * inv * g_ref[:, half:])


def _combine(eidx, rnk, pad_start, w_tok, shared, x1, mod3, g, ys, T):
    N, D = x1.shape
    W = ys.shape[1]
    tm = 256
    per_b = T // tm
    smem = lambda shape, imap: pl.BlockSpec(shape, imap, memory_space=pltpu.SMEM)
    return pl.pallas_call(
        _combine_kernel,
        grid=(N // tm,),
        in_specs=[smem((TOP_K, tm), lambda i: (0, i)), smem((TOP_K, tm), lambda i: (0, i)),
                  pl.BlockSpec(memory_space=pltpu.SMEM),
                  pl.BlockSpec((tm, TOP_K), lambda i: (i, 0)),
                  pl.BlockSpec((tm, D), lambda i: (i, 0)),
                  pl.BlockSpec((tm, D), lambda i: (i, 0)),
                  pl.BlockSpec((1, 6, D), lambda i: (i // per_b, 0, 0)),
                  pl.BlockSpec((1, D), lambda i: (0, 0)),
                  pl.BlockSpec(memory_space=pl.ANY)],
        out_specs=pl.BlockSpec((tm, D), lambda i: (i, 0)),
        out_shape=jax.ShapeDtypeStruct((N, D), F32),
        scratch_shapes=[pltpu.VMEM((TOP_K, tm, W), U32), pltpu.SemaphoreType.DMA],
        compiler_params=_cparams("arbitrary"),
        name="moe_combine",
    )(eidx, rnk, pad_start, w_tok, shared, x1, mod3, g.reshape(1, D), ys)


def _inproj_columns():
    starts = np.concatenate([[0], np.cumsum(IN_SIZES)])
    s_q, s_kv, s_ga, s_qkm, s_vm, s_if, s_om, s_a, s_b = starts[:9]
    cols = np.full((PROJ_W,), D_IN, np.int64)
    cols[OFF_Q:OFF_Q + NSA_Q] = s_q + np.arange(NSA_Q)
    cols[OFF_GATE_A:OFF_GATE_A + D_MODEL] = s_a + np.arange(D_MODEL)
    cols[OFF_GATE_B:OFF_GATE_B + D_MODEL] = s_b + np.arange(D_MODEL)
    for br in range(3):
        for g in range(NSA_KV_GROUPS):
            for kv in range(2):
                dst = OFF_KV + (br * NSA_KV_GROUPS + g) * LANES + kv * NSA_DH
                src = s_kv + (2 * br + kv) * NSA_KV + g * NSA_DH
                cols[dst:dst + NSA_DH] = src + np.arange(NSA_DH)
    cols[OFF_OM:OFF_OM + M_V] = s_om + np.arange(M_V)
    cols[OFF_VM:OFF_VM + M_V] = s_vm + np.arange(M_V)
    cols[OFF_QKM:OFF_QKM + 2 * M_QK] = s_qkm + np.arange(2 * M_QK)
    per_g = 3 * NSA_HPG
    for g in range(NSA_KV_GROUPS):
        cols[OFF_GA + g * LANES:OFF_GA + g * LANES + per_g] = s_ga + g * per_g + np.arange(per_g)
    cols_if = np.full((LANES,), D_IN, np.int64)
    cols_if[:2 * M_HEADS] = s_if + np.arange(2 * M_HEADS)
    return cols, cols_if


def _take_columns(w, cols):
    pieces = []
    start = 0
    for i in range(1, len(cols) + 1):
        run_ends = (i == len(cols) or (cols[start] == D_IN) != (cols[i] == D_IN)
                    or (cols[start] != D_IN and cols[i] != cols[i - 1] + 1))
        if run_ends:
            if cols[start] == D_IN:
                pieces.append(jnp.zeros((w.shape[0], i - start), w.dtype))
            else:
                pieces.append(w[:, int(cols[start]):int(cols[start]) + i - start])
            start = i
    return jnp.concatenate(pieces, axis=1)


def _cmp_to_sel(n_cmp, n_sel):
    ci = np.arange(n_cmp)[:, None]
    sj = np.arange(n_sel)[None, :]
    overlap = (np.minimum(ci * CMP_STRIDE + CMP_LEN, (sj + 1) * SEL_LEN)
               - np.maximum(ci * CMP_STRIDE, sj * SEL_LEN))
    m = np.zeros((LANES, LANES), np.float32)
    m[:n_cmp, :n_sel] = np.clip(overlap, 0, None).astype(np.float32) / CMP_LEN
    return m


def _layer(x, mod3, g_pre_mix, g_post_mix, w_in, cmp_pe, cmp_w1, cmp_w2, conv_w, conv_b, b_gates_m,
           mh_norm_g, w_up_nsa, w_up_mlstm, w_out, g_pre_ffn, g_post_ffn, w_router, b_router,
           w_e_gate, w_e_up, w_e_down, w_sh_gate, w_sh_up, w_sh_down):
    B, T, D = x.shape
    N = B * T
    x2 = x.reshape(N, D)

    cols, cols_if = _inproj_columns()
    w_in_b = w_in.astype(BF16)
    proj, gates = _inproj(x2, mod3, g_pre_mix, _take_columns(w_in_b, cols), _take_columns(w_in_b, cols_if), T)
    proj3 = proj.reshape(B, T, PROJ_W)

    G = NSA_KV_GROUPS
    nb = T // CMP_STRIDE
    kvc_in = proj3[:, :, OFF_KV:OFF_KV + G * LANES].reshape(B, nb, CMP_STRIDE, G, 2, NSA_DH)
    kvc_in = kvc_in.transpose(4, 0, 3, 1, 2, 5).reshape(2, B, G, nb, CMP_STRIDE * NSA_DH)
    pe2 = cmp_pe.reshape(2, 2, CMP_STRIDE * NSA_DH)
    w2x = jnp.stack([jnp.pad(cmp_w2[0], ((0, 0), (0, NSA_DH))),
                     jnp.pad(cmp_w2[1], ((0, 0), (NSA_DH, 0)))]).astype(BF16)
    kvc = _compress(kvc_in[0], kvc_in[1], pe2, cmp_w1.astype(BF16), w2x)

    n_sel = T // SEL_LEN
    c2s = jnp.asarray(_cmp_to_sel(nb - 1, n_sel))
    expand = np.zeros((LANES, T), np.float32)
    expand[np.arange(T) // SEL_LEN, np.arange(T)] = 1.0
    o_nsa = _nsa_attention(proj3, kvc, c2s, jnp.asarray(expand, BF16))

    L = min(MLSTM_CHUNK, T)
    gates3 = gates.reshape(B, T, LANES)
    gates_row = gates3[:, :, :2 * M_HEADS].reshape(B, T // L, L, 2 * M_HEADS).transpose(0, 1, 3, 2)
    o_ml = _mlstm(proj3, gates3, gates_row, conv_w, conv_b, b_gates_m, mh_norm_g)

    merged = _merge(o_nsa.reshape(N, NSA_Q), o_ml.reshape(N, M_V), proj,
                    w_up_nsa.astype(BF16), w_up_mlstm.astype(BF16))
    x1 = _outproj(merged, w_out.astype(BF16), x2, mod3, g_post_mix, T)

    w_router_pad = jnp.pad(w_router, ((0, 0), (0, LANES - N_EXPERTS)))
    w_sh_gu = jnp.concatenate([w_sh_gate, w_sh_up], axis=1).astype(BF16)
    h_packed, logits, shared = _ffn_in(x1, mod3, g_pre_ffn, w_router_pad, w_sh_gu, w_sh_down.astype(BF16), T)
    eidx, rnk, w_k, counts2 = _route(logits, b_router)

    counts = counts2[:, 0]
    padded = (counts + MOE_BLK - 1) // MOE_BLK * MOE_BLK
    pad_end = jnp.cumsum(padded).astype(I32)
    pad_start = pad_end - padded
    n_blk = (N * TOP_K) // MOE_BLK + N_EXPERTS
    blk_exp = jnp.minimum(jnp.searchsorted(pad_end, jnp.arange(n_blk, dtype=I32) * MOE_BLK, side='right'),
                          N_EXPERTS - 1).astype(I32)
    n_used = (pad_end[-1:] // MOE_BLK).astype(I32)

    xs = _dispatch(eidx, rnk, pad_start, counts, pad_end, h_packed, n_blk * MOE_BLK)
    ys = _experts(blk_exp, n_used, xs, w_e_gate.astype(BF16), w_e_up.astype(BF16), w_e_down.astype(BF16))
    out = _combine(eidx, rnk, pad_start, jnp.transpose(w_k), shared, x1, mod3, g_post_ffn, ys, T)
    return out.reshape(B, T, D)


def kernel(x, c, w_ada, b_ada, g_pre_mix, g_post_mix, w_in, cmp_pe, cmp_w1, cmp_w2, conv_w, conv_b, b_gates_m, mh_norm_g, w_up_nsa, w_up_mlstm, w_out, g_pre_ffn, g_post_ffn, w_router, b_router, w_e_gate, w_e_up, w_e_down, w_sh_gate, w_sh_up, w_sh_down):
    B = x.shape[0]
    for l in range(w_ada.shape[0]):
        mod3 = _ada(c, w_ada[l], b_ada[l]).reshape(B, 6, D_MODEL)
        x = _layer(x, mod3, g_pre_mix[l], g_post_mix[l], w_in[l], cmp_pe[l], cmp_w1[l], cmp_w2[l],
                   conv_w[l], conv_b[l], b_gates_m[l], mh_norm_g[l], w_up_nsa[l], w_up_mlstm[l], w_out[l],
                   g_pre_ffn[l], g_post_ffn[l], w_router[l], b_router[l], w_e_gate[l], w_e_up[l],
                   w_e_down[l], w_sh_gate[l], w_sh_up[l], w_sh_down[l])
    return x
```

```python
import functools

import numpy as np
import jax
import jax.numpy as jnp
from jax import lax
from jax.experimental import pallas as pl
from jax.experimental.pallas import tpu as pltpu

F32 = jnp.float32
BF16 = jnp.bfloat16
I32 = jnp.int32
U32 = jnp.uint32

D_MODEL = 2048
NSA_HEADS = 16
NSA_KV_GROUPS = 4
NSA_HPG = NSA_HEADS // NSA_KV_GROUPS
NSA_DH = 64
NSA_Q = NSA_HEADS * NSA_DH
NSA_KV = NSA_KV_GROUPS * NSA_DH
CMP_STRIDE = 16
CMP_LEN = 2 * CMP_STRIDE
CMP_HID = 256
SEL_LEN = 64
SEL_TOP = 16
WIN = 512
FORCE = 1e3
NEG = -1e9
M_HEADS = 4
M_DQK = 128
M_DV = 256
M_QK = M_HEADS * M_DQK
M_V = M_HEADS * M_DV
CONV_W = 4
N_EXPERTS = 64
TOP_K = 8
N_GROUPS = 8
GROUP_SIZE = N_EXPERTS // N_GROUPS
TOP_GROUPS = 4
EXPERT_FF = 512
SHARED_FF = 512
ROUTE_SCALE = 2.5
EPS = 1e-6

IN_SIZES = (NSA_Q, 6 * NSA_KV, 3 * NSA_HEADS, 2 * M_QK, M_V, 2 * M_HEADS, M_V, D_MODEL, D_MODEL)
D_IN = sum(IN_SIZES)

LANES = 128
VMEM_LIMIT = 56 * 1024 * 1024

OFF_Q = 0
OFF_KV = 1024
OFF_OM = 2560
OFF_GA = 3584
OFF_GATE_A = 4096
OFF_GATE_B = 6144
OFF_VM = 8192
OFF_QKM = 9216
PROJ_W = 10240

MLSTM_CHUNK = 256
ATT_TILE = 256
MOE_BLK = 256
ROW_TILE = 8


def _cparams(*sem):
    return pltpu.CompilerParams(dimension_semantics=sem, vmem_limit_bytes=VMEM_LIMIT)


def _dot(a, b, **kw):
    return jnp.dot(a, b, preferred_element_type=F32, **kw)


def _dot_nt(a, b):
    return lax.dot_general(a, b, (((1,), (1,)), ((), ())), preferred_element_type=F32)


def _ada_kernel(c_ref, w_ref, b_ref, o_ref):
    c = c_ref[...]
    sc = (c * jax.nn.sigmoid(c)).astype(BF16)
    o_ref[...] = _dot(sc, w_ref[...].astype(BF16)) + b_ref[...]


def _ada(c, w_ada, b_ada):
    B, D = c.shape
    n_out = w_ada.shape[1]
    tn = 1024
    return pl.pallas_call(
        _ada_kernel,
        grid=(n_out // tn,),
        in_specs=[pl.BlockSpec((B, D), lambda j: (0, 0)),
                  pl.BlockSpec((D, tn), lambda j: (0, j)),
                  pl.BlockSpec((1, tn), lambda j: (0, j))],
        out_specs=pl.BlockSpec((B, tn), lambda j: (0, j)),
        out_shape=jax.ShapeDtypeStruct((B, n_out), F32),
        compiler_params=_cparams("arbitrary"),
        name="ada_mod",
    )(c, w_ada, b_ada.reshape(1, n_out))


def _norm_mod(x, g, shift, scale):
    ms = jnp.mean(x * x, axis=-1, keepdims=True)
    y = x * lax.rsqrt(ms + EPS) * g
    return y * (1.0 + scale) + shift


def _inproj_kernel(x_ref, mod_ref, g_ref, w_ref, wif_ref, o_ref, oif_ref, h_scr):
    @pl.when(pl.program_id(1) == 0)
    def _():
        h = _norm_mod(x_ref[...], g_ref[...], mod_ref[0, 0:1, :], mod_ref[0, 1:2, :])
        hb = h.astype(BF16)
        h_scr[...] = hb
        oif_ref[...] = _dot(hb, wif_ref[...])

    o_ref[...] = _dot(h_scr[...], w_ref[...]).astype(BF16)


def _inproj(x2, mod3, g, w_main, w_if, T):
    N, D = x2.shape
    tm, tn = 512, 1024
    per_b = T // tm
    return pl.pallas_call(
        _inproj_kernel,
        grid=(N // tm, PROJ_W // tn),
        in_specs=[pl.BlockSpec((tm, D), lambda i, j: (i, 0)),
                  pl.BlockSpec((1, 6, D), lambda i, j: (i // per_b, 0, 0)),
                  pl.BlockSpec((1, D), lambda i, j: (0, 0)),
                  pl.BlockSpec((D, tn), lambda i, j: (0, j)),
                  pl.BlockSpec((D, LANES), lambda i, j: (0, 0))],
        out_specs=[pl.BlockSpec((tm, tn), lambda i, j: (i, j)),
                   pl.BlockSpec((tm, LANES), lambda i, j: (i, 0))],
        out_shape=[jax.ShapeDtypeStruct((N, PROJ_W), BF16),
                   jax.ShapeDtypeStruct((N, LANES), F32)],
        scratch_shapes=[pltpu.VMEM((tm, D), BF16)],
        compiler_params=_cparams("arbitrary", "arbitrary"),
        name="in_proj",
    )(x2, mod3, g.reshape(1, D), w_main, w_if)


def _compress_kernel(sk_ref, sv_ref, pe_ref, w1_ref, w2_ref, o_ref):
    nb = sk_ref.shape[2]
    half = CMP_STRIDE * NSA_DH
    out = None
    for kv, s_ref in enumerate((sk_ref, sv_ref)):
        s = s_ref[0, 0].astype(F32)
        top = (s + pe_ref[kv, 0:1, :]).astype(BF16)
        bot = (s + pe_ref[kv, 1:2, :]).astype(BF16)
        a = _dot(top, w1_ref[kv, :half, :])
        b = _dot(bot, w1_ref[kv, half:, :])
        hid = jax.nn.gelu(a + pltpu.roll(b, nb - 1, axis=0))
        y = _dot(hid.astype(BF16), w2_ref[kv])
        out = y if out is None else out + y
    o_ref[0, 0] = out.astype(BF16)


def _compress(sk, sv, pe2, w1, w2x):
    B, G, nb, width = sk.shape
    blk = pl.BlockSpec((1, 1, nb, width), lambda b, g: (b, g, 0, 0))
    return pl.pallas_call(
        _compress_kernel,
        grid=(B, G),
        in_specs=[blk, blk,
                  pl.BlockSpec(pe2.shape, lambda b, g: (0, 0, 0)),
                  pl.BlockSpec(w1.shape, lambda b, g: (0, 0, 0)),
                  pl.BlockSpec(w2x.shape, lambda b, g: (0, 0, 0))],
        out_specs=pl.BlockSpec((1, 1, nb, LANES), lambda b, g: (b, g, 0, 0)),
        out_shape=jax.ShapeDtypeStruct((B, G, nb, LANES), BF16),
        compiler_params=_cparams("arbitrary", "arbitrary"),
        name="nsa_compress",
    )(sk, sv, pe2, w1, w2x)


def _nsa_kernel(q_ref, kvc_ref, kvs_ref, kvw_ref, ga_ref, c2st_ref, o_ref,
                kx_scr, tri_scr, m_scr, l_scr, acc_scr, *, n_cmp, n_sel, top):
    qi = pl.program_id(2)
    tq = q_ref.shape[1]
    tk = tq
    T = kvs_ref.shape[1]
    H = NSA_HPG
    scale = NSA_DH ** -0.5
    n_sel_pad = -(-n_sel // 8) * 8

    @pl.when(qi == 0)
    def _():
        rt = lax.broadcasted_iota(I32, (T, LANES), 0)
        lt = lax.broadcasted_iota(I32, (T, LANES), 1)
        onehot = jnp.where(lt - NSA_DH == rt // SEL_LEN, 1.0, 0.0)
        kx_scr[...] = jnp.where(lt < NSA_DH, kvs_ref[0].astype(F32), onehot).astype(BF16)
        r2 = lax.broadcasted_iota(I32, (H * tq, tk), 0) % tq
        c2 = lax.broadcasted_iota(I32, (H * tq, tk), 1)
        tri_scr[0] = jnp.where(c2 <= r2, 0.0, NEG)
        tri_scr[1] = jnp.where(c2 > r2, 0.0, NEG)

    row = lax.broadcasted_iota(I32, (tq, LANES), 0)
    lane = lax.broadcasted_iota(I32, (tq, LANES), 1)
    t_abs = qi * tq + row
    lo = lane < NSA_DH

    q2 = q_ref[0].astype(F32) * scale
    qh = []
    for pr in range(H // 2):
        qp = q2[:, pr * LANES:(pr + 1) * LANES]
        qh.append(qp)
        qh.append(pltpu.roll(qp, NSA_DH, axis=1))
    qz = jnp.concatenate([jnp.where(lo, x, 0.0) for x in qh], axis=0).astype(BF16)

    kvc = kvc_ref[0, 0]
    cmp_ok = (lane < n_cmp) & (lane * CMP_STRIDE + (CMP_LEN - 1) <= t_abs)
    cmp_ok = jnp.concatenate([cmp_ok] * H, axis=0)
    s = jnp.where(cmp_ok, _dot_nt(qz, kvc), NEG)
    m = jnp.max(s, axis=-1, keepdims=True)
    p = jnp.where(cmp_ok, jnp.exp(s - m), 0.0)
    l = jnp.sum(p, axis=-1, keepdims=True)
    p = p / jnp.where(l > 0.0, l, 1.0)
    acc_scr[2] = _dot(p.astype(BF16), kvc)
    psum = p[0:tq]
    for h in range(1, H):
        psum = psum + p[h * tq:(h + 1) * tq]

    imp_t = lax.dot_general(c2st_ref[...], psum, (((1,), (1,)), ((), ())),
                            preferred_element_type=F32, precision=lax.Precision.HIGHEST)[:n_sel_pad, :]
    blk = lax.broadcasted_iota(I32, (n_sel_pad, tq), 0)
    cur = (qi * tq + lax.broadcasted_iota(I32, (n_sel_pad, tq), 1)) // SEL_LEN
    valid = (blk <= cur) & (blk < n_sel)
    forced = (blk == 0) | (blk == cur) | (blk == cur - 1)
    score = jnp.where(valid, imp_t + jnp.where(forced, FORCE, 0.0), NEG)
    rank = jnp.zeros((n_sel_pad, tq), I32)
    for j in range(n_sel):
        rowv = score[j:j + 1, :]
        rank = rank + ((rowv > score) | ((rowv == score) & (blk > j))).astype(I32)
    nsel_t = jnp.where(valid & (rank < top), 0.0, NEG)
    nsel_t = jnp.concatenate([nsel_t, jnp.zeros((LANES - n_sel_pad, tq), F32)], axis=0)
    nsel = pltpu.roll(jnp.transpose(nsel_t), NSA_DH, axis=1)
    qs = jnp.concatenate([jnp.where(lo, x, nsel) for x in qh], axis=0).astype(BF16)

    m_scr[...] = jnp.full(m_scr.shape, -1e30, F32)
    l_scr[...] = jnp.zeros(l_scr.shape, F32)
    acc_scr[0:2] = jnp.zeros((2, H * tq, LANES), F32)

    def step(br, queries, k_tile, v_tile, bias):
        s = _dot_nt(queries, k_tile)
        if bias is not None:
            s = s + bias
        m_prev = m_scr[br]
        m_new = jnp.maximum(m_prev, jnp.max(s, axis=-1, keepdims=True))
        alpha = jnp.exp(m_prev - m_new)
        p = jnp.exp(s - jnp.concatenate([m_new] * (tk // LANES), axis=1))
        l_scr[br] = alpha * l_scr[br] + jnp.sum(p, axis=-1, keepdims=True)
        acc_scr[br] = alpha * acc_scr[br] + _dot(p.astype(BF16), v_tile)
        m_scr[br] = m_new

    def sel_tile(kt, carry):
        off = pl.multiple_of(kt * tk, tk)
        step(0, qs, kx_scr[pl.ds(off, tk), :], kvs_ref[0, pl.ds(off, tk), :], None)
        return carry
    lax.fori_loop(0, qi, sel_tile, 0)

    def win_tile(back, bias):
        off = pl.multiple_of((qi - back) * tk, tk)
        w = kvw_ref[0, pl.ds(off, tk), :]
        step(1, qz, w, w, bias)

    @pl.when(qi >= 2)
    def _():
        win_tile(2, tri_scr[1])

    @pl.when(qi >= 1)
    def _():
        win_tile(1, None)

    diag = pl.multiple_of(qi * tk, tk)
    step(0, qs, kx_scr[pl.ds(diag, tk), :], kvs_ref[0, pl.ds(diag, tk), :], tri_scr[0])
    win_tile(0, tri_scr[0])

    gates = jax.nn.sigmoid(ga_ref[0].astype(F32))

    def gate_rows(br):
        return jnp.concatenate([jnp.broadcast_to(gates[:, 3 * h + br:3 * h + br + 1], (tq, LANES))
                                for h in range(H)], axis=0)
    o_all = (gate_rows(0) * acc_scr[2] + gate_rows(1) * (acc_scr[0] / l_scr[0])
             + gate_rows(2) * (acc_scr[1] / l_scr[1]))
    for pr in range(H // 2):
        even = o_all[2 * pr * tq:(2 * pr + 1) * tq]
        odd = o_all[(2 * pr + 1) * tq:(2 * pr + 2) * tq]
        pair = jnp.where(lo, pltpu.roll(even, NSA_DH, axis=1), odd)
        o_ref[0, :, pr * LANES:(pr + 1) * LANES] = pair.astype(BF16)


def _nsa_attention(proj3, kvc, c2s_t):
    B, T, _ = proj3.shape
    G = NSA_KV_GROUPS
    tq = min(ATT_TILE, T)
    nq = T // tq
    n_cmp = T // CMP_STRIDE - 1
    n_sel = T // SEL_LEN
    top = min(SEL_TOP, n_sel)
    gw = NSA_HPG * NSA_DH
    assert kvc.shape[2] <= LANES and n_sel <= NSA_DH and WIN == 2 * tq
    kvc = jnp.pad(kvc, ((0, 0), (0, 0), (0, LANES - kvc.shape[2]), (0, 0)))
    kern = functools.partial(_nsa_kernel, n_cmp=n_cmp, n_sel=n_sel, top=top)
    return pl.pallas_call(
        kern,
        grid=(B, G, nq),
        in_specs=[
            pl.BlockSpec((1, tq, gw), lambda b, g, i: (b, i, OFF_Q // gw + g)),
            pl.BlockSpec((1, 1, LANES, LANES), lambda b, g, i: (b, g, 0, 0)),
            pl.BlockSpec((1, T, LANES), lambda b, g, i: (b, 0, OFF_KV // LANES + G + g)),
            pl.BlockSpec((1, T, LANES), lambda b, g, i: (b, 0, OFF_KV // LANES + 2 * G + g)),
            pl.BlockSpec((1, tq, LANES), lambda b, g, i: (b, i, OFF_GA // LANES + g)),
            pl.BlockSpec((LANES, LANES), lambda b, g, i: (0, 0)),
        ],
        out_specs=pl.BlockSpec((1, tq, gw), lambda b, g, i: (b, i, g)),
        out_shape=jax.ShapeDtypeStruct((B, T, NSA_Q), BF16),
        scratch_shapes=[pltpu.VMEM((T, LANES), BF16),
                        pltpu.VMEM((2, NSA_HPG * tq, tq), F32),
                        pltpu.VMEM((2, NSA_HPG * tq, LANES), F32),
                        pltpu.VMEM((2, NSA_HPG * tq, LANES), F32),
                        pltpu.VMEM((3, NSA_HPG * tq, LANES), F32)],
        compiler_params=_cparams("arbitrary", "arbitrary", "arbitrary"),
        name="nsa_attention",
    )(proj3, kvc, proj3, proj3, proj3, c2s_t)


def _log_sigmoid(x):
    return -(jnp.maximum(-x, 0.0) + jnp.log1p(jnp.exp(-jnp.abs(x))))


def _mlstm_kernel(q_ref, k_ref, v_ref, o_ref, gc_ref, gr_ref, cwq_ref, cwk_ref, cbq_ref, cbk_ref,
                  bg_ref, ng_ref, out_ref, q_scr, k_scr, c_scr):
    hd = pl.program_id(1)
    T = q_ref.shape[1]
    L = gr_ref.shape[3]
    nc = T // L
    dv = v_ref.shape[2]

    trow = lax.broadcasted_iota(I32, (T, M_DQK), 0)

    def conv_silu(x_ref, w_ref, b_ref):
        x = x_ref[0].astype(F32)
        y = b_ref[...] + w_ref[CONV_W - 1:CONV_W, :] * x
        for d in range(1, CONV_W):
            xs = jnp.where(trow >= d, pltpu.roll(x, d, axis=0), 0.0)
            y = y + w_ref[CONV_W - 1 - d:CONV_W - d, :] * xs
        return y * jax.nn.sigmoid(y)

    q_scr[...] = conv_silu(q_ref, cwq_ref, cbq_ref).astype(BF16)
    k_scr[...] = (conv_silu(k_ref, cwk_ref, cbk_ref) * (M_DQK ** -0.5)).astype(BF16)
    c_scr[...] = jnp.zeros_like(c_scr)

    lane_l = lax.broadcasted_iota(I32, (L, LANES), 1)
    sub8 = lax.broadcasted_iota(I32, (2 * M_HEADS, L), 0)
    r_i = lax.broadcasted_iota(I32, (L, L), 0)
    c_i = lax.broadcasted_iota(I32, (L, L), 1)
    tri = c_i <= r_i
    ones_blk = jnp.where(lane_l == 0, 1.0, 0.0).astype(BF16)

    bl = lax.broadcasted_iota(I32, (1, LANES), 1)
    bg = bg_ref[...]
    b_i = jnp.sum(jnp.where(bl == hd, bg, 0.0), axis=-1, keepdims=True)
    b_f = jnp.sum(jnp.where(bl == hd + M_HEADS, bg, 0.0), axis=-1, keepdims=True)

    def chunk(c, m):
        off = pl.multiple_of(c * L, L)
        gc = gc_ref[0, pl.ds(off, L), :]
        gr = gr_ref[0, c]
        li_col = jnp.sum(jnp.where(lane_l == hd, gc, 0.0), axis=-1, keepdims=True) + b_i
        lf_col = _log_sigmoid(jnp.sum(jnp.where(lane_l == hd + M_HEADS, gc, 0.0), axis=-1, keepdims=True) + b_f)
        li_row = jnp.sum(jnp.where(sub8 == hd, gr, 0.0), axis=0, keepdims=True) + b_i
        lf_row = _log_sigmoid(jnp.sum(jnp.where(sub8 == hd + M_HEADS, gr, 0.0), axis=0, keepdims=True) + b_f)
        a_col = jnp.sum(jnp.where(tri, lf_row, 0.0), axis=-1, keepdims=True)
        a_row = jnp.sum(jnp.where(r_i <= c_i, lf_col, 0.0), axis=0, keepdims=True)
        a_end = jnp.sum(lf_row, axis=-1, keepdims=True)

        dlog = jnp.where(tri, a_col - a_row + li_row, -jnp.inf)
        inter = a_col + m
        mt = jnp.maximum(inter, jnp.max(dlog, axis=-1, keepdims=True))
        dmat = jnp.exp(dlog - mt)
        iw = jnp.exp(inter - mt)

        qc = q_scr[pl.ds(off, L), :]
        kc = k_scr[pl.ds(off, L), :]
        v_aug = jnp.concatenate([v_ref[0, pl.ds(off, L), :], ones_blk], axis=-1)
        s = (_dot_nt(qc, kc) * dmat).astype(BF16)
        cmat = c_scr[...]
        num = iw * _dot(qc, cmat.astype(BF16)) + _dot(s, v_aug)
        den = num[:, dv:dv + 1]
        hc = num[:, :dv] / jnp.maximum(jnp.abs(den), jnp.exp(-mt))

        elog = a_end - a_col + li_col
        m_new = jnp.maximum(a_end + m, jnp.max(elog, axis=0, keepdims=True))
        ew = jnp.exp(elog - m_new)
        decay = jnp.exp(a_end + m - m_new)
        kct = jnp.transpose(kc.astype(F32)).astype(BF16)
        c_scr[...] = decay * cmat + _dot(kct, (ew * v_aug.astype(F32)).astype(BF16))

        hn = hc * lax.rsqrt(jnp.mean(hc * hc, axis=-1, keepdims=True) + EPS) * ng_ref[...]
        og = jax.nn.sigmoid(o_ref[0, pl.ds(off, L), :].astype(F32))
        out_ref[0, pl.ds(off, L), :] = (og * hn).astype(BF16)
        return m_new

    lax.fori_loop(0, nc, chunk, jnp.zeros((1, 1), F32))


def _mlstm(proj3, gates_col, gates_row, conv_w, conv_b, b_gates, norm_g):
    B, T, _ = proj3.shape
    L = gates_row.shape[3]
    nc = T // L
    H = M_HEADS
    qb = OFF_QKM // M_DQK
    return pl.pallas_call(
        _mlstm_kernel,
        grid=(B, H),
        in_specs=[
            pl.BlockSpec((1, T, M_DQK), lambda b, h: (b, 0, qb + h)),
            pl.BlockSpec((1, T, M_DQK), lambda b, h: (b, 0, qb + H + h)),
            pl.BlockSpec((1, T, M_DV), lambda b, h: (b, 0, OFF_VM // M_DV + h)),
            pl.BlockSpec((1, T, M_DV), lambda b, h: (b, 0, OFF_OM // M_DV + h)),
            pl.BlockSpec((1, T, LANES), lambda b, h: (b, 0, 0)),
            pl.BlockSpec((1, nc, 2 * H, L), lambda b, h: (b, 0, 0, 0)),
            pl.BlockSpec((CONV_W, M_DQK), lambda b, h: (0, h)),
            pl.BlockSpec((CONV_W, M_DQK), lambda b, h: (0, H + h)),
            pl.BlockSpec((1, M_DQK), lambda b, h: (0, h)),
            pl.BlockSpec((1, M_DQK), lambda b, h: (0, H + h)),
            pl.BlockSpec((1, LANES), lambda b, h: (0, 0)),
            pl.BlockSpec((1, M_DV), lambda b, h: (0, h)),
        ],
        out_specs=pl.BlockSpec((1, T, M_DV), lambda b, h: (b, 0, h)),
        out_shape=jax.ShapeDtypeStruct((B, T, M_V), BF16),
        scratch_shapes=[pltpu.VMEM((T, M_DQK), BF16), pltpu.VMEM((T, M_DQK), BF16),
                        pltpu.VMEM((M_DQK, M_DV + LANES), F32)],
        compiler_params=_cparams("arbitrary", "arbitrary"),
        name="mlstm",
    )(proj3, proj3, proj3, proj3, gates_col, gates_row, conv_w, conv_w,
      conv_b.reshape(1, -1), conv_b.reshape(1, -1),
      jnp.pad(b_gates, (0, LANES - 2 * H)).reshape(1, LANES), norm_g.reshape(1, -1))


def _merge_kernel(on_ref, om_ref, ga_ref, gb_ref, wn_ref, wm_ref, o_ref):
    a = _dot(on_ref[...], wn_ref[...])
    b = _dot(om_ref[...], wm_ref[...])
    ga = jax.nn.sigmoid(ga_ref[...].astype(F32))
    gb = jax.nn.sigmoid(gb_ref[...].astype(F32))
    o_ref[...] = (ga * a + gb * b).astype(BF16)


def _merge(o_nsa, o_ml, proj, w_up_nsa, w_up_ml):
    N = o_nsa.shape[0]
    D = D_MODEL
    tm = 512
    return pl.pallas_call(
        _merge_kernel,
        grid=(N // tm,),
        in_specs=[pl.BlockSpec((tm, NSA_Q), lambda i: (i, 0)),
                  pl.BlockSpec((tm, M_V), lambda i: (i, 0)),
                  pl.BlockSpec((tm, D), lambda i: (i, OFF_GATE_A // D)),
                  pl.BlockSpec((tm, D), lambda i: (i, OFF_GATE_B // D)),
                  pl.BlockSpec((NSA_Q, D), lambda i: (0, 0)),
                  pl.BlockSpec((M_V, D), lambda i: (0, 0))],
        out_specs=pl.BlockSpec((tm, D), lambda i: (i, 0)),
        out_shape=jax.ShapeDtypeStruct((N, D), BF16),
        compiler_params=_cparams("arbitrary"),
        name="mixer_merge",
    )(o_nsa, o_ml, proj, proj, w_up_nsa, w_up_ml)


def _outproj_kernel(m_ref, w_ref, x_ref, mod_ref, g_ref, o_ref):
    y = _dot(m_ref[...], w_ref[...])
    ms = jnp.mean(y * y, axis=-1, keepdims=True)
    o_ref[...] = x_ref[...] + mod_ref[0, 2:3, :] * (y * lax.rsqrt(ms + EPS) * g_ref[...])


def _outproj(merged, w_out, x2, mod3, g, T):
    N, D = x2.shape
    tm = 512
    per_b = T // tm
    return pl.pallas_call(
        _outproj_kernel,
        grid=(N // tm,),
        in_specs=[pl.BlockSpec((tm, D), lambda i: (i, 0)),
                  pl.BlockSpec((D, D), lambda i: (0, 0)),
                  pl.BlockSpec((tm, D), lambda i: (i, 0)),
                  pl.BlockSpec((1, 6, D), lambda i: (i // per_b, 0, 0)),
                  pl.BlockSpec((1, D), lambda i: (0, 0))],
        out_specs=pl.BlockSpec((tm, D), lambda i: (i, 0)),
        out_shape=jax.ShapeDtypeStruct((N, D), F32),
        compiler_params=_cparams("arbitrary"),
        name="out_proj",
    )(merged, w_out, x2, mod3, g.reshape(1, D))


def _pack_rows(y):
    half = y.shape[1] // 2
    return pltpu.pack_elementwise([y[:, :half], y[:, half:]], packed_dtype=BF16)


def _unpack_rows(w):
    lo = pltpu.unpack_elementwise(w, index=0, packed_dtype=BF16, unpacked_dtype=F32)
    hi = pltpu.unpack_elementwise(w, index=1, packed_dtype=BF16, unpacked_dtype=F32)
    return lo, hi


def _store_rows(ref, packed):
    m = packed.shape[0]
    for c in range(ROW_TILE):
        ref[pl.ds(c, m, stride=ROW_TILE), :] = packed[:, c * LANES:(c + 1) * LANES]


def _store_zero_rows(ref):
    zero = jnp.zeros(ref.shape, F32)
    ref[...] = pltpu.pack_elementwise([zero, zero], packed_dtype=BF16)


def _load_row_chunk(ref, c, m):
    return ref[pl.ds(c, m, stride=ROW_TILE), :]


def _ffn_in_kernel(x_ref, mod_ref, g_ref, wr_ref, wgu_ref, wd_ref, hp_ref, lg_ref, sh_ref):
    h = _norm_mod(x_ref[...], g_ref[...], mod_ref[0, 3:4, :], mod_ref[0, 4:5, :])
    lg_ref[...] = _dot(h, wr_ref[...], precision=lax.Precision.HIGHEST)
    hb = h.astype(BF16)
    _store_rows(hp_ref, _pack_rows(h))
    gu = _dot(hb, wgu_ref[...])
    ff = gu.shape[1] // 2
    gate, up = gu[:, :ff], gu[:, ff:]
    act = (gate * jax.nn.sigmoid(gate) * up).astype(BF16)
    sh_ref[...] = _dot(act, wd_ref[...]).astype(BF16)


def _ffn_in(x1, mod3, g, w_router_pad, w_sh_gu, w_sh_down, T):
    N, D = x1.shape
    assert D // 2 == ROW_TILE * LANES
    tm = 512
    per_b = T // tm
    return pl.pallas_call(
        _ffn_in_kernel,
        grid=(N // tm,),
        in_specs=[pl.BlockSpec((tm, D), lambda i: (i, 0)),
                  pl.BlockSpec((1, 6, D), lambda i: (i // per_b, 0, 0)),
                  pl.BlockSpec((1, D), lambda i: (0, 0)),
                  pl.BlockSpec((D, LANES), lambda i: (0, 0)),
                  pl.BlockSpec(w_sh_gu.shape, lambda i: (0, 0)),
                  pl.BlockSpec(w_sh_down.shape, lambda i: (0, 0))],
        out_specs=[pl.BlockSpec((tm * ROW_TILE, LANES), lambda i: (i, 0)),
                   pl.BlockSpec((tm, LANES), lambda i: (i, 0)),
                   pl.BlockSpec((tm, D), lambda i: (i, 0))],
        out_shape=[jax.ShapeDtypeStruct((N * ROW_TILE, LANES), U32),
                   jax.ShapeDtypeStruct((N, LANES), F32),
                   jax.ShapeDtypeStruct((N, D), BF16)],
        compiler_params=_cparams("arbitrary"),
        name="ffn_in",
    )(x1, mod3, g.reshape(1, D), w_router_pad, w_sh_gu, w_sh_down)


def _route_kernel(lg_ref, b_ref, tri_ref, e_ref, r_ref, w_ref, cnt_ref, run_scr):
    tm = lg_ref.shape[0]
    E, GS, NG = N_EXPERTS, GROUP_SIZE, N_GROUPS

    @pl.when(pl.program_id(0) == 0)
    def _():
        run_scr[...] = jnp.zeros_like(run_scr)

    s = jax.nn.sigmoid(jnp.transpose(lg_ref[...])[:E, :])
    sb = s + b_ref[:, 0:1]
    gi = lax.broadcasted_iota(I32, (NG, tm), 0)
    gs = jnp.zeros((NG, tm), F32)
    for j in range(NG):
        blk = sb[j * GS:(j + 1) * GS, :]
        m1 = jnp.max(blk, axis=0, keepdims=True)
        first = jnp.min(jnp.where(blk == m1, gi, GS), axis=0, keepdims=True)
        m2 = jnp.max(jnp.where(gi == first, -jnp.inf, blk), axis=0, keepdims=True)
        gs = jnp.where(gi == j, m1 + m2, gs)

    grank = jnp.zeros((NG, tm), I32)
    for j in range(NG):
        rowv = gs[j:j + 1, :]
        grank = grank + ((rowv > gs) | ((rowv == gs) & (gi > j))).astype(I32)
    gsel = jnp.where(grank < TOP_GROUPS, 1.0, 0.0)

    ei = lax.broadcasted_iota(I32, (E, tm), 0)
    esel = jnp.zeros((E, tm), F32)
    for j in range(NG):
        esel = jnp.where(ei // GS == j, gsel[j:j + 1, :], esel)
    masked = jnp.where(esel > 0.5, sb, NEG)

    erank = jnp.zeros((E, tm), I32)
    for j in range(E):
        rowv = masked[j:j + 1, :]
        erank = erank + ((rowv > masked) | ((rowv == masked) & (ei > j))).astype(I32)
    selm = erank < TOP_K

    wsum = jnp.sum(jnp.where(selm, s, 0.0), axis=0, keepdims=True)
    wfull = jnp.where(selm, s / wsum * ROUTE_SCALE, 0.0)

    self32 = selm.astype(F32)
    before = _dot(self32.astype(BF16), tri_ref[...]) + run_scr[:, 0:1]
    run_new = run_scr[:, 0:1] + jnp.sum(self32, axis=1, keepdims=True)
    run_scr[...] = jnp.broadcast_to(run_new, run_scr.shape)
    cnt_ref[...] = jnp.broadcast_to(run_new, cnt_ref.shape).astype(I32)

    before_i = before.astype(I32)
    for k in range(TOP_K):
        hit = selm & (erank == k)
        e_ref[k:k + 1, :] = jnp.sum(jnp.where(hit, ei, 0), axis=0, keepdims=True)
        r_ref[k:k + 1, :] = jnp.sum(jnp.where(hit, before_i, 0), axis=0, keepdims=True)
        w_ref[k:k + 1, :] = jnp.sum(jnp.where(hit, wfull, 0.0), axis=0, keepdims=True)


def _route(logits, b_router):
    N = logits.shape[0]
    tm = 256
    tri = jnp.asarray(np.triu(np.ones((tm, tm), np.float32), 1), BF16)
    b_col = jnp.broadcast_to(b_router.reshape(N_EXPERTS, 1), (N_EXPERTS, LANES))
    kspec = pl.BlockSpec((TOP_K, tm), lambda i: (0, i))
    return pl.pallas_call(
        _route_kernel,
        grid=(N // tm,),
        in_specs=[pl.BlockSpec((tm, LANES), lambda i: (i, 0)),
                  pl.BlockSpec((N_EXPERTS, LANES), lambda i: (0, 0)),
                  pl.BlockSpec((tm, tm), lambda i: (0, 0))],
        out_specs=[kspec, kspec, kspec, pl.BlockSpec((N_EXPERTS, LANES), lambda i: (0, 0))],
        out_shape=[jax.ShapeDtypeStruct((TOP_K, N), I32), jax.ShapeDtypeStruct((TOP_K, N), I32),
                   jax.ShapeDtypeStruct((TOP_K, N), F32), jax.ShapeDtypeStruct((N_EXPERTS, LANES), I32)],
        scratch_shapes=[pltpu.VMEM((N_EXPERTS, LANES), F32)],
        compiler_params=_cparams("arbitrary"),
        name="moe_route",
    )(logits, b_col, tri)


def _dispatch_kernel(dst_ref, ps_ref, cnt_ref, pe_ref, h_ref, xs_ref, zero_scr, sem, zsem):
    i = pl.program_id(0)
    tm = h_ref.shape[0] // ROW_TILE

    def tile_at(ref, r):
        return ref.at[pl.ds(pl.multiple_of(r * ROW_TILE, ROW_TILE), ROW_TILE), :]

    @pl.when(i == 0)
    def _():
        _store_zero_rows(zero_scr)

        def per_expert(e, n):
            lo = ps_ref[e] + cnt_ref[e]
            hi = pe_ref[e]

            def issue(rw, c):
                pltpu.make_async_copy(tile_at(zero_scr, 0), tile_at(xs_ref, rw), zsem).start()
                return c
            lax.fori_loop(lo, hi, issue, 0)
            return n + (hi - lo)
        n_pad = lax.fori_loop(0, N_EXPERTS, per_expert, 0)

        def drain(_, c):
            pltpu.make_async_copy(tile_at(zero_scr, 0), tile_at(xs_ref, 0), zsem).wait()
            return c
        lax.fori_loop(0, n_pad, drain, 0)

        blk_rows = zero_scr.shape[0]

        def tail_copy(b):
            return pltpu.make_async_copy(
                zero_scr, xs_ref.at[pl.ds(pl.multiple_of(b * blk_rows, blk_rows), blk_rows), :], zsem)
        first_free = pe_ref[N_EXPERTS - 1] * ROW_TILE // blk_rows
        n_blocks = xs_ref.shape[0] // blk_rows
        lax.fori_loop(first_free, n_blocks, lambda b, c: (tail_copy(b).start(), c)[1], 0)
        lax.fori_loop(first_free, n_blocks, lambda b, c: (tail_copy(b).wait(), c)[1], 0)

    def issue_row(r, c):
        src = tile_at(h_ref, r)
        for k in range(TOP_K):
            pltpu.make_async_copy(src, tile_at(xs_ref, dst_ref[k, r]), sem).start()
        return c
    lax.fori_loop(0, tm, issue_row, 0)

    for k in range(TOP_K):
        pltpu.make_async_copy(h_ref, xs_ref.at[pl.ds(0, tm * ROW_TILE), :], sem).wait()


def _dispatch(dest, pad_start, counts, pad_end, h_packed, n_rows):
    N = h_packed.shape[0] // ROW_TILE
    tm = 256
    whole = lambda: pl.BlockSpec(memory_space=pltpu.SMEM)
    return pl.pallas_call(
        _dispatch_kernel,
        grid=(N // tm,),
        in_specs=[pl.BlockSpec((TOP_K, tm), lambda i: (0, i), memory_space=pltpu.SMEM),
                  whole(), whole(), whole(),
                  pl.BlockSpec((tm * ROW_TILE, LANES), lambda i: (i, 0))],
        out_specs=pl.BlockSpec(memory_space=pl.ANY),
        out_shape=jax.ShapeDtypeStruct((n_rows * ROW_TILE, LANES), U32),
        scratch_shapes=[pltpu.VMEM((MOE_BLK * ROW_TILE, LANES), U32),
                        pltpu.SemaphoreType.DMA, pltpu.SemaphoreType.DMA],
        compiler_params=_cparams("arbitrary"),
        name="moe_dispatch",
    )(dest, pad_start, counts, pad_end, h_packed)


def _expert_kernel(bexp_ref, nused_ref, xs_ref, wg_ref, wu_ref, wd_ref, ys_ref, wg_scr, wu_scr, wd_scr):
    i = pl.program_id(0)
    used = i < nused_ref[0]

    @pl.when(jnp.logical_not(used))
    def _():
        _store_zero_rows(ys_ref)

    @pl.when(used & ((i == 0) | (bexp_ref[i] != bexp_ref[jnp.maximum(i - 1, 0)])))
    def _():
        wg_scr[...] = wg_ref[0].astype(BF16)
        wu_scr[...] = wu_ref[0].astype(BF16)
        wd_scr[...] = wd_ref[0].astype(BF16)

    @pl.when(used)
    def _():
        m = xs_ref.shape[0] // ROW_TILE
        words = jnp.concatenate([_load_row_chunk(xs_ref, c, m) for c in range(ROW_TILE)], axis=1)
        lo, hi = _unpack_rows(words)
        lo, hi = lo.astype(BF16), hi.astype(BF16)
        half = lo.shape[1]
        gate = _dot(lo, wg_scr[:half, :]) + _dot(hi, wg_scr[half:, :])
        up = _dot(lo, wu_scr[:half, :]) + _dot(hi, wu_scr[half:, :])
        act = (gate * jax.nn.sigmoid(gate) * up).astype(BF16)
        _store_rows(ys_ref, _pack_rows(_dot(act, wd_scr[...])))


def _experts(blk_exp, n_used, xs, w_gate, w_up, w_down):
    P = xs.shape[0] // ROW_TILE
    n_blk = P // MOE_BLK
    D, FF = w_gate.shape[1], w_gate.shape[2]
    row_map = lambda i, be, nu: (jnp.minimum(i, nu[0] - 1), 0)
    return pl.pallas_call(
        _expert_kernel,
        grid_spec=pltpu.PrefetchScalarGridSpec(
            num_scalar_prefetch=2,
            grid=(n_blk,),
            in_specs=[pl.BlockSpec((MOE_BLK * ROW_TILE, LANES), row_map),
                      pl.BlockSpec((1, D, FF), lambda i, be, nu: (be[i], 0, 0)),
                      pl.BlockSpec((1, D, FF), lambda i, be, nu: (be[i], 0, 0)),
                      pl.BlockSpec((1, FF, D), lambda i, be, nu: (be[i], 0, 0))],
            out_specs=pl.BlockSpec((MOE_BLK * ROW_TILE, LANES), lambda i, be, nu: (i, 0)),
            scratch_shapes=[pltpu.VMEM((D, FF), BF16), pltpu.VMEM((D, FF), BF16), pltpu.VMEM((FF, D), BF16)]),
        out_shape=jax.ShapeDtypeStruct(xs.shape, U32),
        compiler_params=_cparams("arbitrary"),
        name="moe_experts",
    )(blk_exp, n_used, xs, w_gate, w_up, w_down)


def _combine_kernel(dcur_ref, dnxt_ref, w_ref, sh_ref, x_ref, mod_ref, g_ref, ys_ref, o_ref, buf2, sems):
    i = pl.program_id(0)
    n = pl.num_programs(0)
    tm = x_ref.shape[0]
    half = ROW_TILE * LANES

    def tile_at(ref, r):
        return ref.at[pl.ds(pl.multiple_of(r * ROW_TILE, ROW_TILE), ROW_TILE), :]

    def gather(slots_ref, slot):
        def issue_row(r, c):
            for k in range(TOP_K):
                pltpu.make_async_copy(tile_at(ys_ref, slots_ref[k, r]),
                                      tile_at(buf2.at[slot, k], r), sems.at[slot]).start()
            return c
        lax.fori_loop(0, tm, issue_row, 0)

    @pl.when(i == 0)
    def _():
        gather(dcur_ref, 0)

    @pl.when(i + 1 < n)
    def _():
        gather(dnxt_ref, (i + 1) % 2)

    slot = i % 2
    buf = buf2.at[slot]
    for k in range(TOP_K):
        pltpu.make_async_copy(ys_ref.at[pl.ds(0, tm * ROW_TILE), :], buf.at[k], sems.at[slot]).wait()

    w = w_ref[...]
    ssq = jnp.zeros((tm, 1), F32)
    for c in range(ROW_TILE):
        c_lo = slice(c * LANES, (c + 1) * LANES)
        c_hi = slice(half + c * LANES, half + (c + 1) * LANES)
        acc_lo = sh_ref[:, c_lo].astype(F32)
        acc_hi = sh_ref[:, c_hi].astype(F32)
        for k in range(TOP_K):
            lo, hi = _unpack_rows(_load_row_chunk(buf.at[k], c, tm))
            wk = w[:, k:k + 1]
            acc_lo = acc_lo + wk * lo
            acc_hi = acc_hi + wk * hi
        o_ref[:, c_lo] = acc_lo
        o_ref[:, c_hi] = acc_hi
        ssq = ssq + jnp.sum(acc_lo * acc_lo, axis=-1, keepdims=True) + jnp.sum(acc_hi * acc_hi, axis=-1, keepdims=True)
    inv = lax.rsqrt(ssq / (2 * half) + EPS)
    o_ref[...] = x_ref[...] + mod_ref[0, 5:6, :] * (o_ref[...] * inv * g_ref[...])


def _combine(dest, w_tok, shared, x1, mod3, g, ys, T):
    N, D = x1.shape
    tm = 256
    per_b = T // tm
    last = N // tm - 1
    return pl.pallas_call(
        _combine_kernel,
        grid=(N // tm,),
        in_specs=[pl.BlockSpec((TOP_K, tm), lambda i: (0, i), memory_space=pltpu.SMEM),
                  pl.BlockSpec((TOP_K, tm), lambda i: (0, jnp.minimum(i + 1, last)), memory_space=pltpu.SMEM),
                  pl.BlockSpec((tm, TOP_K), lambda i: (i, 0)),
                  pl.BlockSpec((tm, D), lambda i: (i, 0)),
                  pl.BlockSpec((tm, D), lambda i: (i, 0)),
                  pl.BlockSpec((1, 6, D), lambda i: (i // per_b, 0, 0)),
                  pl.BlockSpec((1, D), lambda i: (0, 0)),
                  pl.BlockSpec(memory_space=pl.ANY)],
        out_specs=pl.BlockSpec((tm, D), lambda i: (i, 0)),
        out_shape=jax.ShapeDtypeStruct((N, D), F32),
        scratch_shapes=[pltpu.VMEM((2, TOP_K, tm * ROW_TILE, LANES), U32), pltpu.SemaphoreType.DMA((2,))],
        compiler_params=_cparams("arbitrary"),
        name="moe_combine",
    )(dest, dest, w_tok, shared, x1, mod3, g.reshape(1, D), ys)


def _inproj_columns():
    starts = np.concatenate([[0], np.cumsum(IN_SIZES)])
    s_q, s_kv, s_ga, s_qkm, s_vm, s_if, s_om, s_a, s_b = starts[:9]
    cols = np.full((PROJ_W,), D_IN, np.int64)
    cols[OFF_Q:OFF_Q + NSA_Q] = s_q + np.arange(NSA_Q)
    cols[OFF_GATE_A:OFF_GATE_A + D_MODEL] = s_a + np.arange(D_MODEL)
    cols[OFF_GATE_B:OFF_GATE_B + D_MODEL] = s_b + np.arange(D_MODEL)
    for br in range(3):
        for g in range(NSA_KV_GROUPS):
            for kv in range(2):
                dst = OFF_KV + (br * NSA_KV_GROUPS + g) * LANES + kv * NSA_DH
                src = s_kv + (2 * br + kv) * NSA_KV + g * NSA_DH
                cols[dst:dst + NSA_DH] = src + np.arange(NSA_DH)
    cols[OFF_OM:OFF_OM + M_V] = s_om + np.arange(M_V)
    cols[OFF_VM:OFF_VM + M_V] = s_vm + np.arange(M_V)
    cols[OFF_QKM:OFF_QKM + 2 * M_QK] = s_qkm + np.arange(2 * M_QK)
    per_g = 3 * NSA_HPG
    for g in range(NSA_KV_GROUPS):
        cols[OFF_GA + g * LANES:OFF_GA + g * LANES + per_g] = s_ga + g * per_g + np.arange(per_g)
    cols_if = np.full((LANES,), D_IN, np.int64)
    cols_if[:2 * M_HEADS] = s_if + np.arange(2 * M_HEADS)
    return cols, cols_if


def _take_columns(w, cols):
    pieces = []
    start = 0
    for i in range(1, len(cols) + 1):
        run_ends = (i == len(cols) or (cols[start] == D_IN) != (cols[i] == D_IN)
                    or (cols[start] != D_IN and cols[i] != cols[i - 1] + 1))
        if run_ends:
            if cols[start] == D_IN:
                pieces.append(jnp.zeros((w.shape[0], i - start), w.dtype))
            else:
                pieces.append(w[:, int(cols[start]):int(cols[start]) + i - start])
            start = i
    return jnp.concatenate(pieces, axis=1)


def _cmp_to_sel(n_cmp, n_sel):
    ci = np.arange(n_cmp)[:, None]
    sj = np.arange(n_sel)[None, :]
    overlap = (np.minimum(ci * CMP_STRIDE + CMP_LEN, (sj + 1) * SEL_LEN)
               - np.maximum(ci * CMP_STRIDE, sj * SEL_LEN))
    m = np.zeros((LANES, LANES), np.float32)
    m[:n_cmp, :n_sel] = np.clip(overlap, 0, None).astype(np.float32) / CMP_LEN
    return m


def _layer(x, mod3, g_pre_mix, g_post_mix, w_in, cmp_pe, cmp_w1, cmp_w2, conv_w, conv_b, b_gates_m,
           mh_norm_g, w_up_nsa, w_up_mlstm, w_out, g_pre_ffn, g_post_ffn, w_router, b_router,
           w_e_gate, w_e_up, w_e_down, w_sh_gate, w_sh_up, w_sh_down):
    B, T, D = x.shape
    N = B * T
    x2 = x.reshape(N, D)

    cols, cols_if = _inproj_columns()
    w_in_b = w_in.astype(BF16)
    proj, gates = _inproj(x2, mod3, g_pre_mix, _take_columns(w_in_b, cols), _take_columns(w_in_b, cols_if), T)
    proj3 = proj.reshape(B, T, PROJ_W)

    G = NSA_KV_GROUPS
    nb = T // CMP_STRIDE
    kvc_in = proj3[:, :, OFF_KV:OFF_KV + G * LANES].reshape(B, nb, CMP_STRIDE, G, 2, NSA_DH)
    kvc_in = kvc_in.transpose(4, 0, 3, 1, 2, 5).reshape(2, B, G, nb, CMP_STRIDE * NSA_DH)
    pe2 = cmp_pe.reshape(2, 2, CMP_STRIDE * NSA_DH)
    w2x = jnp.stack([jnp.pad(cmp_w2[0], ((0, 0), (0, NSA_DH))),
                     jnp.pad(cmp_w2[1], ((0, 0), (NSA_DH, 0)))]).astype(BF16)
    kvc = _compress(kvc_in[0], kvc_in[1], pe2, cmp_w1.astype(BF16), w2x)

    n_sel = T // SEL_LEN
    o_nsa = _nsa_attention(proj3, kvc, jnp.asarray(_cmp_to_sel(nb - 1, n_sel).T))

    L = min(MLSTM_CHUNK, T)
    gates3 = gates.reshape(B, T, LANES)
    gates_row = gates3[:, :, :2 * M_HEADS].reshape(B, T // L, L, 2 * M_HEADS).transpose(0, 1, 3, 2)
    o_ml = _mlstm(proj3, gates3, gates_row, conv_w, conv_b, b_gates_m, mh_norm_g)

    merged = _merge(o_nsa.reshape(N, NSA_Q), o_ml.reshape(N, M_V), proj,
                    w_up_nsa.astype(BF16), w_up_mlstm.astype(BF16))
    x1 = _outproj(merged, w_out.astype(BF16), x2, mod3, g_post_mix, T)

    w_router_pad = jnp.pad(w_router, ((0, 0), (0, LANES - N_EXPERTS)))
    w_sh_gu = jnp.concatenate([w_sh_gate, w_sh_up], axis=1).astype(BF16)
    h_packed, logits, shared = _ffn_in(x1, mod3, g_pre_ffn, w_router_pad, w_sh_gu, w_sh_down.astype(BF16), T)
    eidx, rnk, w_k, counts2 = _route(logits, b_router)

    counts = counts2[:, 0]
    padded = (counts + MOE_BLK - 1) // MOE_BLK * MOE_BLK
    pad_end = jnp.cumsum(padded).astype(I32)
    pad_start = pad_end - padded
    n_blk = (N * TOP_K) // MOE_BLK + N_EXPERTS
    blk_first = jnp.arange(n_blk, dtype=I32) * MOE_BLK
    blk_exp = jnp.minimum(jnp.sum((pad_end[None, :] <= blk_first[:, None]).astype(I32), axis=1), N_EXPERTS - 1)
    n_used = (pad_end[-1:] // MOE_BLK).astype(I32)
    experts = jnp.arange(N_EXPERTS, dtype=I32)
    dest = rnk + jnp.sum(jnp.where(eidx[:, :, None] == experts, pad_start, 0), axis=-1)

    xs = _dispatch(dest, pad_start, counts, pad_end, h_packed, n_blk * MOE_BLK)
    ys = _experts(blk_exp, n_used, xs, w_e_gate, w_e_up, w_e_down)
    out = _combine(dest, jnp.transpose(w_k), shared, x1, mod3, g_post_ffn, ys, T)
    return out.reshape(B, T, D)


def kernel(x, c, w_ada, b_ada, g_pre_mix, g_post_mix, w_in, cmp_pe, cmp_w1, cmp_w2, conv_w, conv_b, b_gates_m, mh_norm_g, w_up_nsa, w_up_mlstm, w_out, g_pre_ffn, g_post_ffn, w_router, b_router, w_e_gate, w_e_up, w_e_down, w_sh_gate, w_sh_up, w_sh_down):
    B = x.shape[0]
    for l in range(w_ada.shape[0]):
        mod3 = _ada(c, w_ada[l], b_ada[l]).reshape(B, 6, D_MODEL)
        x = _layer(x, mod3, g_pre_mix[l], g_post_mix[l], w_in[l], cmp_pe[l], cmp_w1[l], cmp_w2[l],
                   conv_w[l], conv_b[l], b_gates_m[l], mh_norm_g[l], w_up_nsa[l], w_up_mlstm[l], w_out[l],
                   g_pre_ffn[l], g_post_ffn[l], w_router[l], b_router[l], w_e_gate[l], w_e_up[l],
                   w_e_down[l], w_sh_gate[l], w_sh_up[l], w_sh_down[l])
    return x
```

```python
import functools

import numpy as np
import jax
import jax.numpy as jnp
from jax import lax
from jax.experimental import pallas as pl
from jax.experimental.pallas import tpu as pltpu

F32 = jnp.float32
BF16 = jnp.bfloat16
I32 = jnp.int32
U32 = jnp.uint32

D_MODEL = 2048
NSA_HEADS = 16
NSA_KV_GROUPS = 4
NSA_HPG = NSA_HEADS // NSA_KV_GROUPS
NSA_DH = 64
NSA_Q = NSA_HEADS * NSA_DH
NSA_KV = NSA_KV_GROUPS * NSA_DH
CMP_STRIDE = 16
CMP_LEN = 2 * CMP_STRIDE
CMP_HID = 256
SEL_LEN = 64
SEL_TOP = 16
WIN = 512
FORCE = 1e3
NEG = -1e9
M_HEADS = 4
M_DQK = 128
M_DV = 256
M_QK = M_HEADS * M_DQK
M_V = M_HEADS * M_DV
CONV_W = 4
N_EXPERTS = 64
TOP_K = 8
N_GROUPS = 8
GROUP_SIZE = N_EXPERTS // N_GROUPS
TOP_GROUPS = 4
EXPERT_FF = 512
SHARED_FF = 512
ROUTE_SCALE = 2.5
EPS = 1e-6

IN_SIZES = (NSA_Q, 6 * NSA_KV, 3 * NSA_HEADS, 2 * M_QK, M_V, 2 * M_HEADS, M_V, D_MODEL, D_MODEL)
D_IN = sum(IN_SIZES)

LANES = 128
VMEM_LIMIT = 56 * 1024 * 1024

OFF_Q = 0
OFF_KV = 1024
OFF_OM = 2560
OFF_GA = 3584
OFF_GATE_A = 4096
OFF_GATE_B = 6144
OFF_VM = 8192
OFF_QKM = 9216
PROJ_W = 10240

MLSTM_CHUNK = 256
ATT_TILE = 256
MOE_BLK = 256
ROW_TILE = 8


def _cparams(*sem):
    return pltpu.CompilerParams(dimension_semantics=sem, vmem_limit_bytes=VMEM_LIMIT)


def _dot(a, b, **kw):
    return jnp.dot(a, b, preferred_element_type=F32, **kw)


def _dot_nt(a, b):
    return lax.dot_general(a, b, (((1,), (1,)), ((), ())), preferred_element_type=F32)


def _ada_kernel(c_ref, w_ref, b_ref, o_ref):
    c = c_ref[...]
    sc = (c * jax.nn.sigmoid(c)).astype(BF16)
    o_ref[...] = _dot(sc, w_ref[...].astype(BF16)) + b_ref[...]


def _ada(c, w_ada, b_ada):
    B, D = c.shape
    n_out = w_ada.shape[1]
    tn = 1024
    return pl.pallas_call(
        _ada_kernel,
        grid=(n_out // tn,),
        in_specs=[pl.BlockSpec((B, D), lambda j: (0, 0)),
                  pl.BlockSpec((D, tn), lambda j: (0, j)),
                  pl.BlockSpec((1, tn), lambda j: (0, j))],
        out_specs=pl.BlockSpec((B, tn), lambda j: (0, j)),
        out_shape=jax.ShapeDtypeStruct((B, n_out), F32),
        compiler_params=_cparams("arbitrary"),
        name="ada_mod",
    )(c, w_ada, b_ada.reshape(1, n_out))


def _norm_mod(x, g, shift, scale):
    ms = jnp.mean(x * x, axis=-1, keepdims=True)
    y = x * lax.rsqrt(ms + EPS) * g
    return y * (1.0 + scale) + shift


def _inproj_kernel(x_ref, mod_ref, g_ref, w_ref, wif_ref, o_ref, oif_ref, h_scr):
    @pl.when(pl.program_id(1) == 0)
    def _():
        h = _norm_mod(x_ref[...], g_ref[...], mod_ref[0, 0:1, :], mod_ref[0, 1:2, :])
        hb = h.astype(BF16)
        h_scr[...] = hb
        oif_ref[...] = _dot(hb, wif_ref[...])

    o_ref[...] = _dot(h_scr[...], w_ref[...]).astype(BF16)


def _inproj(x2, mod3, g, w_main, w_if, T):
    N, D = x2.shape
    tm, tn = min(1024, T), 1024
    per_b = T // tm
    return pl.pallas_call(
        _inproj_kernel,
        grid=(N // tm, PROJ_W // tn),
        in_specs=[pl.BlockSpec((tm, D), lambda i, j: (i, 0)),
                  pl.BlockSpec((1, 6, D), lambda i, j: (i // per_b, 0, 0)),
                  pl.BlockSpec((1, D), lambda i, j: (0, 0)),
                  pl.BlockSpec((D, tn), lambda i, j: (0, j)),
                  pl.BlockSpec((D, LANES), lambda i, j: (0, 0))],
        out_specs=[pl.BlockSpec((tm, tn), lambda i, j: (i, j)),
                   pl.BlockSpec((tm, LANES), lambda i, j: (i, 0))],
        out_shape=[jax.ShapeDtypeStruct((N, PROJ_W), BF16),
                   jax.ShapeDtypeStruct((N, LANES), F32)],
        scratch_shapes=[pltpu.VMEM((tm, D), BF16)],
        compiler_params=_cparams("arbitrary", "arbitrary"),
        name="in_proj",
    )(x2, mod3, g.reshape(1, D), w_main, w_if)


def _compress_kernel(sk_ref, sv_ref, pe_ref, w1_ref, w2_ref, o_ref):
    nb = sk_ref.shape[2]
    half = CMP_STRIDE * NSA_DH
    out = None
    for kv, s_ref in enumerate((sk_ref, sv_ref)):
        s = s_ref[0, 0].astype(F32)
        top = (s + pe_ref[kv, 0:1, :]).astype(BF16)
        bot = (s + pe_ref[kv, 1:2, :]).astype(BF16)
        a = _dot(top, w1_ref[kv, :half, :])
        b = _dot(bot, w1_ref[kv, half:, :])
        hid = jax.nn.gelu(a + pltpu.roll(b, nb - 1, axis=0))
        y = _dot(hid.astype(BF16), w2_ref[kv])
        out = y if out is None else out + y
    o_ref[0, 0] = out.astype(BF16)


def _compress(sk, sv, pe2, w1, w2x):
    B, G, nb, width = sk.shape
    blk = pl.BlockSpec((1, 1, nb, width), lambda b, g: (b, g, 0, 0))
    return pl.pallas_call(
        _compress_kernel,
        grid=(B, G),
        in_specs=[blk, blk,
                  pl.BlockSpec(pe2.shape, lambda b, g: (0, 0, 0)),
                  pl.BlockSpec(w1.shape, lambda b, g: (0, 0, 0)),
                  pl.BlockSpec(w2x.shape, lambda b, g: (0, 0, 0))],
        out_specs=pl.BlockSpec((1, 1, nb, LANES), lambda b, g: (b, g, 0, 0)),
        out_shape=jax.ShapeDtypeStruct((B, G, nb, LANES), BF16),
        compiler_params=_cparams("arbitrary", "arbitrary"),
        name="nsa_compress",
    )(sk, sv, pe2, w1, w2x)


def _nsa_kernel(q_ref, kvc_ref, kvs_ref, kvw_ref, ga_ref, c2st_ref, o_ref,
                kx_scr, tri_scr, m_scr, l_scr, acc_scr, *, n_cmp, n_sel, top):
    qi = pl.program_id(2)
    tq = q_ref.shape[1]
    tk = tq
    T = kvs_ref.shape[1]
    H = NSA_HPG
    scale = NSA_DH ** -0.5
    n_sel_pad = -(-n_sel // 8) * 8

    @pl.when(qi == 0)
    def _():
        rt = lax.broadcasted_iota(I32, (T, LANES), 0)
        lt = lax.broadcasted_iota(I32, (T, LANES), 1)
        onehot = jnp.where(lt - NSA_DH == rt // SEL_LEN, 1.0, 0.0)
        kx_scr[...] = jnp.where(lt < NSA_DH, kvs_ref[0].astype(F32), onehot).astype(BF16)
        r2 = lax.broadcasted_iota(I32, (H * tq, tk), 0) % tq
        c2 = lax.broadcasted_iota(I32, (H * tq, tk), 1)
        tri_scr[0] = jnp.where(c2 <= r2, 0.0, NEG)
        tri_scr[1] = jnp.where(c2 > r2, 0.0, NEG)

    row = lax.broadcasted_iota(I32, (tq, LANES), 0)
    lane = lax.broadcasted_iota(I32, (tq, LANES), 1)
    t_abs = qi * tq + row
    lo = lane < NSA_DH

    q2 = q_ref[0].astype(F32) * scale
    qh = []
    for pr in range(H // 2):
        qp = q2[:, pr * LANES:(pr + 1) * LANES]
        qh.append(qp)
        qh.append(pltpu.roll(qp, NSA_DH, axis=1))
    qz = jnp.concatenate([jnp.where(lo, x, 0.0) for x in qh], axis=0).astype(BF16)

    kvc = kvc_ref[0, 0]
    cmp_ok = (lane < n_cmp) & (lane * CMP_STRIDE + (CMP_LEN - 1) <= t_abs)
    cmp_ok = jnp.concatenate([cmp_ok] * H, axis=0)
    s = jnp.where(cmp_ok, _dot_nt(qz, kvc), NEG)
    m = jnp.max(s, axis=-1, keepdims=True)
    p = jnp.where(cmp_ok, jnp.exp(s - m), 0.0)
    l = jnp.sum(p, axis=-1, keepdims=True)
    p = p / jnp.where(l > 0.0, l, 1.0)
    acc_scr[2] = _dot(p.astype(BF16), kvc)
    psum = p[0:tq]
    for h in range(1, H):
        psum = psum + p[h * tq:(h + 1) * tq]

    imp_t = lax.dot_general(c2st_ref[...], psum, (((1,), (1,)), ((), ())),
                            preferred_element_type=F32, precision=lax.Precision.HIGHEST)[:n_sel_pad, :]
    blk = lax.broadcasted_iota(I32, (n_sel_pad, tq), 0)
    cur = (qi * tq + lax.broadcasted_iota(I32, (n_sel_pad, tq), 1)) // SEL_LEN
    valid = (blk <= cur) & (blk < n_sel)
    forced = (blk == 0) | (blk == cur) | (blk == cur - 1)
    score = jnp.where(valid, imp_t + jnp.where(forced, FORCE, 0.0), NEG)
    rank = jnp.zeros((n_sel_pad, tq), I32)
    for j in range(n_sel):
        rowv = score[j:j + 1, :]
        rank = rank + ((rowv > score) | ((rowv == score) & (blk > j))).astype(I32)
    nsel_t = jnp.where(valid & (rank < top), 0.0, NEG)
    nsel_t = jnp.concatenate([nsel_t, jnp.zeros((LANES - n_sel_pad, tq), F32)], axis=0)
    nsel = pltpu.roll(jnp.transpose(nsel_t), NSA_DH, axis=1)
    qs = jnp.concatenate([jnp.where(lo, x, nsel) for x in qh], axis=0).astype(BF16)

    m_scr[...] = jnp.full(m_scr.shape, -1e30, F32)
    l_scr[...] = jnp.zeros(l_scr.shape, F32)
    acc_scr[0:2] = jnp.zeros((2, H * tq, LANES), F32)

    def step(br, queries, k_tile, v_tile, bias):
        s = _dot_nt(queries, k_tile)
        if bias is not None:
            s = s + bias
        m_prev = m_scr[br]
        m_new = jnp.maximum(m_prev, jnp.max(s, axis=-1, keepdims=True))
        alpha = jnp.exp(m_prev - m_new)
        p = jnp.exp(s - jnp.concatenate([m_new] * (tk // LANES), axis=1))
        l_scr[br] = alpha * l_scr[br] + jnp.sum(p, axis=-1, keepdims=True)
        acc_scr[br] = alpha * acc_scr[br] + _dot(p.astype(BF16), v_tile)
        m_scr[br] = m_new

    def sel_tile(kt, carry):
        off = pl.multiple_of(kt * tk, tk)
        step(0, qs, kx_scr[pl.ds(off, tk), :], kvs_ref[0, pl.ds(off, tk), :], None)
        return carry
    lax.fori_loop(0, qi, sel_tile, 0)

    def win_tile(back, bias):
        off = pl.multiple_of((qi - back) * tk, tk)
        w = kvw_ref[0, pl.ds(off, tk), :]
        step(1, qz, w, w, bias)

    @pl.when(qi >= 2)
    def _():
        win_tile(2, tri_scr[1])

    @pl.when(qi >= 1)
    def _():
        win_tile(1, None)

    diag = pl.multiple_of(qi * tk, tk)
    step(0, qs, kx_scr[pl.ds(diag, tk), :], kvs_ref[0, pl.ds(diag, tk), :], tri_scr[0])
    win_tile(0, tri_scr[0])

    gates = jax.nn.sigmoid(ga_ref[0].astype(F32))

    def gate_rows(br):
        return jnp.concatenate([jnp.broadcast_to(gates[:, 3 * h + br:3 * h + br + 1], (tq, LANES))
                                for h in range(H)], axis=0)
    o_all = (gate_rows(0) * acc_scr[2] + gate_rows(1) * (acc_scr[0] / l_scr[0])
             + gate_rows(2) * (acc_scr[1] / l_scr[1]))
    for pr in range(H // 2):
        even = o_all[2 * pr * tq:(2 * pr + 1) * tq]
        odd = o_all[(2 * pr + 1) * tq:(2 * pr + 2) * tq]
        pair = jnp.where(lo, pltpu.roll(even, NSA_DH, axis=1), odd)
        o_ref[0, :, pr * LANES:(pr + 1) * LANES] = pair.astype(BF16)


def _nsa_attention(proj3, kvc, c2s_t):
    B, T, _ = proj3.shape
    G = NSA_KV_GROUPS
    tq = min(ATT_TILE, T)
    nq = T // tq
    n_cmp = T // CMP_STRIDE - 1
    n_sel = T // SEL_LEN
    top = min(SEL_TOP, n_sel)
    gw = NSA_HPG * NSA_DH
    assert kvc.shape[2] <= LANES and n_sel <= NSA_DH and WIN == 2 * tq
    kvc = jnp.pad(kvc, ((0, 0), (0, 0), (0, LANES - kvc.shape[2]), (0, 0)))
    kern = functools.partial(_nsa_kernel, n_cmp=n_cmp, n_sel=n_sel, top=top)
    return pl.pallas_call(
        kern,
        grid=(B, G, nq),
        in_specs=[
            pl.BlockSpec((1, tq, gw), lambda b, g, i: (b, i, OFF_Q // gw + g)),
            pl.BlockSpec((1, 1, LANES, LANES), lambda b, g, i: (b, g, 0, 0)),
            pl.BlockSpec((1, T, LANES), lambda b, g, i: (b, 0, OFF_KV // LANES + G + g)),
            pl.BlockSpec((1, T, LANES), lambda b, g, i: (b, 0, OFF_KV // LANES + 2 * G + g)),
            pl.BlockSpec((1, tq, LANES), lambda b, g, i: (b, i, OFF_GA // LANES + g)),
            pl.BlockSpec((LANES, LANES), lambda b, g, i: (0, 0)),
        ],
        out_specs=pl.BlockSpec((1, tq, gw), lambda b, g, i: (b, i, g)),
        out_shape=jax.ShapeDtypeStruct((B, T, NSA_Q), BF16),
        scratch_shapes=[pltpu.VMEM((T, LANES), BF16),
                        pltpu.VMEM((2, NSA_HPG * tq, tq), F32),
                        pltpu.VMEM((2, NSA_HPG * tq, LANES), F32),
                        pltpu.VMEM((2, NSA_HPG * tq, LANES), F32),
                        pltpu.VMEM((3, NSA_HPG * tq, LANES), F32)],
        compiler_params=_cparams("arbitrary", "arbitrary", "arbitrary"),
        name="nsa_attention",
    )(proj3, kvc, proj3, proj3, proj3, c2s_t)


def _log_sigmoid(x):
    return -(jnp.maximum(-x, 0.0) + jnp.log1p(jnp.exp(-jnp.abs(x))))


def _mlstm_kernel(q_ref, k_ref, v_ref, o_ref, gc_ref, gr_ref, cwq_ref, cwk_ref, cbq_ref, cbk_ref,
                  bg_ref, ng_ref, out_ref, q_scr, k_scr, c_scr):
    hd = pl.program_id(1)
    T = q_ref.shape[1]
    L = gr_ref.shape[3]
    nc = T // L
    dv = v_ref.shape[2]

    trow = lax.broadcasted_iota(I32, (T, M_DQK), 0)

    def conv_silu(x_ref, w_ref, b_ref):
        x = x_ref[0].astype(F32)
        y = b_ref[...] + w_ref[CONV_W - 1:CONV_W, :] * x
        for d in range(1, CONV_W):
            xs = jnp.where(trow >= d, pltpu.roll(x, d, axis=0), 0.0)
            y = y + w_ref[CONV_W - 1 - d:CONV_W - d, :] * xs
        return y * jax.nn.sigmoid(y)

    q_scr[...] = conv_silu(q_ref, cwq_ref, cbq_ref).astype(BF16)
    k_scr[...] = (conv_silu(k_ref, cwk_ref, cbk_ref) * (M_DQK ** -0.5)).astype(BF16)
    c_scr[...] = jnp.zeros_like(c_scr)

    lane_l = lax.broadcasted_iota(I32, (L, LANES), 1)
    sub8 = lax.broadcasted_iota(I32, (2 * M_HEADS, L), 0)
    r_i = lax.broadcasted_iota(I32, (L, L), 0)
    c_i = lax.broadcasted_iota(I32, (L, L), 1)
    tri = c_i <= r_i
    ones_blk = jnp.where(lane_l == 0, 1.0, 0.0).astype(BF16)

    bl = lax.broadcasted_iota(I32, (1, LANES), 1)
    bg = bg_ref[...]
    b_i = jnp.sum(jnp.where(bl == hd, bg, 0.0), axis=-1, keepdims=True)
    b_f = jnp.sum(jnp.where(bl == hd + M_HEADS, bg, 0.0), axis=-1, keepdims=True)

    def chunk(c, m):
        off = pl.multiple_of(c * L, L)
        gc = gc_ref[0, pl.ds(off, L), :]
        gr = gr_ref[0, c]
        li_col = jnp.sum(jnp.where(lane_l == hd, gc, 0.0), axis=-1, keepdims=True) + b_i
        lf_col = _log_sigmoid(jnp.sum(jnp.where(lane_l == hd + M_HEADS, gc, 0.0), axis=-1, keepdims=True) + b_f)
        li_row = jnp.sum(jnp.where(sub8 == hd, gr, 0.0), axis=0, keepdims=True) + b_i
        lf_row = _log_sigmoid(jnp.sum(jnp.where(sub8 == hd + M_HEADS, gr, 0.0), axis=0, keepdims=True) + b_f)
        a_col = jnp.sum(jnp.where(tri, lf_row, 0.0), axis=-1, keepdims=True)
        a_row = jnp.sum(jnp.where(r_i <= c_i, lf_col, 0.0), axis=0, keepdims=True)
        a_end = jnp.sum(lf_row, axis=-1, keepdims=True)

        dlog = jnp.where(tri, a_col - a_row + li_row, -jnp.inf)
        inter = a_col + m
        mt = jnp.maximum(inter, jnp.max(dlog, axis=-1, keepdims=True))
        dmat = jnp.exp(dlog - mt)
        iw = jnp.exp(inter - mt)

        qc = q_scr[pl.ds(off, L), :]
        kc = k_scr[pl.ds(off, L), :]
        v_aug = jnp.concatenate([v_ref[0, pl.ds(off, L), :], ones_blk], axis=-1)
        s = (_dot_nt(qc, kc) * dmat).astype(BF16)
        cmat = c_scr[...]
        num = iw * _dot(qc, cmat.astype(BF16)) + _dot(s, v_aug)
        den = num[:, dv:dv + 1]
        hc = num[:, :dv] / jnp.maximum(jnp.abs(den), jnp.exp(-mt))

        elog = a_end - a_col + li_col
        m_new = jnp.maximum(a_end + m, jnp.max(elog, axis=0, keepdims=True))
        ew = jnp.exp(elog - m_new)
        decay = jnp.exp(a_end + m - m_new)
        kct = jnp.transpose(kc.astype(F32)).astype(BF16)
        c_scr[...] = decay * cmat + _dot(kct, (ew * v_aug.astype(F32)).astype(BF16))

        hn = hc * lax.rsqrt(jnp.mean(hc * hc, axis=-1, keepdims=True) + EPS) * ng_ref[...]
        og = jax.nn.sigmoid(o_ref[0, pl.ds(off, L), :].astype(F32))
        out_ref[0, pl.ds(off, L), :] = (og * hn).astype(BF16)
        return m_new

    lax.fori_loop(0, nc, chunk, jnp.zeros((1, 1), F32))


def _mlstm(proj3, gates_col, gates_row, conv_w, conv_b, b_gates, norm_g):
    B, T, _ = proj3.shape
    L = gates_row.shape[3]
    nc = T // L
    H = M_HEADS
    qb = OFF_QKM // M_DQK
    return pl.pallas_call(
        _mlstm_kernel,
        grid=(B, H),
        in_specs=[
            pl.BlockSpec((1, T, M_DQK), lambda b, h: (b, 0, qb + h)),
            pl.BlockSpec((1, T, M_DQK), lambda b, h: (b, 0, qb + H + h)),
            pl.BlockSpec((1, T, M_DV), lambda b, h: (b, 0, OFF_VM // M_DV + h)),
            pl.BlockSpec((1, T, M_DV), lambda b, h: (b, 0, OFF_OM // M_DV + h)),
            pl.BlockSpec((1, T, LANES), lambda b, h: (b, 0, 0)),
            pl.BlockSpec((1, nc, 2 * H, L), lambda b, h: (b, 0, 0, 0)),
            pl.BlockSpec((CONV_W, M_DQK), lambda b, h: (0, h)),
            pl.BlockSpec((CONV_W, M_DQK), lambda b, h: (0, H + h)),
            pl.BlockSpec((1, M_DQK), lambda b, h: (0, h)),
            pl.BlockSpec((1, M_DQK), lambda b, h: (0, H + h)),
            pl.BlockSpec((1, LANES), lambda b, h: (0, 0)),
            pl.BlockSpec((1, M_DV), lambda b, h: (0, h)),
        ],
        out_specs=pl.BlockSpec((1, T, M_DV), lambda b, h: (b, 0, h)),
        out_shape=jax.ShapeDtypeStruct((B, T, M_V), BF16),
        scratch_shapes=[pltpu.VMEM((T, M_DQK), BF16), pltpu.VMEM((T, M_DQK), BF16),
                        pltpu.VMEM((M_DQK, M_DV + LANES), F32)],
        compiler_params=_cparams("arbitrary", "arbitrary"),
        name="mlstm",
    )(proj3, proj3, proj3, proj3, gates_col, gates_row, conv_w, conv_w,
      conv_b.reshape(1, -1), conv_b.reshape(1, -1),
      jnp.pad(b_gates, (0, LANES - 2 * H)).reshape(1, LANES), norm_g.reshape(1, -1))


def _merge_kernel(on_ref, om_ref, ga_ref, gb_ref, wn_ref, wm_ref, o_ref):
    a = _dot(on_ref[...], wn_ref[...])
    b = _dot(om_ref[...], wm_ref[...])
    ga = jax.nn.sigmoid(ga_ref[...].astype(F32))
    gb = jax.nn.sigmoid(gb_ref[...].astype(F32))
    o_ref[...] = (ga * a + gb * b).astype(BF16)


def _merge(o_nsa, o_ml, proj, w_up_nsa, w_up_ml):
    N = o_nsa.shape[0]
    D = D_MODEL
    tm = 512
    return pl.pallas_call(
        _merge_kernel,
        grid=(N // tm,),
        in_specs=[pl.BlockSpec((tm, NSA_Q), lambda i: (i, 0)),
                  pl.BlockSpec((tm, M_V), lambda i: (i, 0)),
                  pl.BlockSpec((tm, D), lambda i: (i, OFF_GATE_A // D)),
                  pl.BlockSpec((tm, D), lambda i: (i, OFF_GATE_B // D)),
                  pl.BlockSpec((NSA_Q, D), lambda i: (0, 0)),
                  pl.BlockSpec((M_V, D), lambda i: (0, 0))],
        out_specs=pl.BlockSpec((tm, D), lambda i: (i, 0)),
        out_shape=jax.ShapeDtypeStruct((N, D), BF16),
        compiler_params=_cparams("arbitrary"),
        name="mixer_merge",
    )(o_nsa, o_ml, proj, proj, w_up_nsa, w_up_ml)


def _outproj_kernel(m_ref, w_ref, x_ref, mod_ref, g_ref, o_ref):
    y = _dot(m_ref[...], w_ref[...])
    ms = jnp.mean(y * y, axis=-1, keepdims=True)
    o_ref[...] = x_ref[...] + mod_ref[0, 2:3, :] * (y * lax.rsqrt(ms + EPS) * g_ref[...])


def _outproj(merged, w_out, x2, mod3, g, T):
    N, D = x2.shape
    tm = 512
    per_b = T // tm
    return pl.pallas_call(
        _outproj_kernel,
        grid=(N // tm,),
        in_specs=[pl.BlockSpec((tm, D), lambda i: (i, 0)),
                  pl.BlockSpec((D, D), lambda i: (0, 0)),
                  pl.BlockSpec((tm, D), lambda i: (i, 0)),
                  pl.BlockSpec((1, 6, D), lambda i: (i // per_b, 0, 0)),
                  pl.BlockSpec((1, D), lambda i: (0, 0))],
        out_specs=pl.BlockSpec((tm, D), lambda i: (i, 0)),
        out_shape=jax.ShapeDtypeStruct((N, D), F32),
        compiler_params=_cparams("arbitrary"),
        name="out_proj",
    )(merged, w_out, x2, mod3, g.reshape(1, D))


def _pack_rows(y):
    half = y.shape[1] // 2
    return pltpu.pack_elementwise([y[:, :half], y[:, half:]], packed_dtype=BF16)


def _unpack_rows(w):
    lo = pltpu.unpack_elementwise(w, index=0, packed_dtype=BF16, unpacked_dtype=F32)
    hi = pltpu.unpack_elementwise(w, index=1, packed_dtype=BF16, unpacked_dtype=F32)
    return lo, hi


def _store_rows(ref, packed):
    m = packed.shape[0]
    for c in range(ROW_TILE):
        ref[pl.ds(c, m, stride=ROW_TILE), :] = packed[:, c * LANES:(c + 1) * LANES]


def _store_zero_rows(ref):
    zero = jnp.zeros(ref.shape, F32)
    ref[...] = pltpu.pack_elementwise([zero, zero], packed_dtype=BF16)


def _load_row_chunk(ref, c, m):
    return ref[pl.ds(c, m, stride=ROW_TILE), :]


def _ffn_in_kernel(x_ref, mod_ref, g_ref, wr_ref, hp_ref, lg_ref):
    h = _norm_mod(x_ref[...], g_ref[...], mod_ref[0, 3:4, :], mod_ref[0, 4:5, :])
    lg_ref[...] = _dot(h, wr_ref[...], precision=lax.Precision.HIGHEST)
    _store_rows(hp_ref, _pack_rows(h))


def _ffn_in(x1, mod3, g, w_router_pad, T):
    N, D = x1.shape
    assert D // 2 == ROW_TILE * LANES
    tm = 512
    per_b = T // tm
    return pl.pallas_call(
        _ffn_in_kernel,
        grid=(N // tm,),
        in_specs=[pl.BlockSpec((tm, D), lambda i: (i, 0)),
                  pl.BlockSpec((1, 6, D), lambda i: (i // per_b, 0, 0)),
                  pl.BlockSpec((1, D), lambda i: (0, 0)),
                  pl.BlockSpec((D, LANES), lambda i: (0, 0))],
        out_specs=[pl.BlockSpec((tm * ROW_TILE, LANES), lambda i: (i, 0)),
                   pl.BlockSpec((tm, LANES), lambda i: (i, 0))],
        out_shape=[jax.ShapeDtypeStruct((N * ROW_TILE, LANES), U32),
                   jax.ShapeDtypeStruct((N, LANES), F32)],
        compiler_params=_cparams("arbitrary"),
        name="ffn_in",
    )(x1, mod3, g.reshape(1, D), w_router_pad)


def _route_kernel(lg_ref, b_ref, tri_ref, e_ref, r_ref, w_ref, cnt_ref, run_scr):
    tm = lg_ref.shape[0]
    E, GS, NG = N_EXPERTS, GROUP_SIZE, N_GROUPS

    @pl.when(pl.program_id(0) == 0)
    def _():
        run_scr[...] = jnp.zeros_like(run_scr)

    s = jax.nn.sigmoid(jnp.transpose(lg_ref[...])[:E, :])
    sb = s + b_ref[:, 0:1]
    gi = lax.broadcasted_iota(I32, (NG, tm), 0)
    gs = jnp.zeros((NG, tm), F32)
    for j in range(NG):
        blk = sb[j * GS:(j + 1) * GS, :]
        m1 = jnp.max(blk, axis=0, keepdims=True)
        first = jnp.min(jnp.where(blk == m1, gi, GS), axis=0, keepdims=True)
        m2 = jnp.max(jnp.where(gi == first, -jnp.inf, blk), axis=0, keepdims=True)
        gs = jnp.where(gi == j, m1 + m2, gs)

    grank = jnp.zeros((NG, tm), I32)
    for j in range(NG):
        rowv = gs[j:j + 1, :]
        grank = grank + ((rowv > gs) | ((rowv == gs) & (gi > j))).astype(I32)
    gsel = jnp.where(grank < TOP_GROUPS, 1.0, 0.0)

    ei = lax.broadcasted_iota(I32, (E, tm), 0)
    esel = jnp.zeros((E, tm), F32)
    for j in range(NG):
        esel = jnp.where(ei // GS == j, gsel[j:j + 1, :], esel)
    masked = jnp.where(esel > 0.5, sb, NEG)

    erank = jnp.zeros((E, tm), I32)
    for j in range(E):
        rowv = masked[j:j + 1, :]
        erank = erank + ((rowv > masked) | ((rowv == masked) & (ei > j))).astype(I32)
    selm = erank < TOP_K

    wsum = jnp.sum(jnp.where(selm, s, 0.0), axis=0, keepdims=True)
    wfull = jnp.where(selm, s / wsum * ROUTE_SCALE, 0.0)

    self32 = selm.astype(F32)
    before = _dot(self32.astype(BF16), tri_ref[...]) + run_scr[:, 0:1]
    run_new = run_scr[:, 0:1] + jnp.sum(self32, axis=1, keepdims=True)
    run_scr[...] = jnp.broadcast_to(run_new, run_scr.shape)
    cnt_ref[...] = jnp.broadcast_to(run_new, cnt_ref.shape).astype(I32)

    before_i = before.astype(I32)
    for k in range(TOP_K):
        hit = selm & (erank == k)
        e_ref[k:k + 1, :] = jnp.sum(jnp.where(hit, ei, 0), axis=0, keepdims=True)
        r_ref[k:k + 1, :] = jnp.sum(jnp.where(hit, before_i, 0), axis=0, keepdims=True)
        w_ref[k:k + 1, :] = jnp.sum(jnp.where(hit, wfull, 0.0), axis=0, keepdims=True)


def _route(logits, b_router):
    N = logits.shape[0]
    tm = 256
    tri = jnp.asarray(np.triu(np.ones((tm, tm), np.float32), 1), BF16)
    b_col = jnp.broadcast_to(b_router.reshape(N_EXPERTS, 1), (N_EXPERTS, LANES))
    kspec = pl.BlockSpec((TOP_K, tm), lambda i: (0, i))
    return pl.pallas_call(
        _route_kernel,
        grid=(N // tm,),
        in_specs=[pl.BlockSpec((tm, LANES), lambda i: (i, 0)),
                  pl.BlockSpec((N_EXPERTS, LANES), lambda i: (0, 0)),
                  pl.BlockSpec((tm, tm), lambda i: (0, 0))],
        out_specs=[kspec, kspec, kspec, pl.BlockSpec((N_EXPERTS, LANES), lambda i: (0, 0))],
        out_shape=[jax.ShapeDtypeStruct((TOP_K, N), I32), jax.ShapeDtypeStruct((TOP_K, N), I32),
                   jax.ShapeDtypeStruct((TOP_K, N), F32), jax.ShapeDtypeStruct((N_EXPERTS, LANES), I32)],
        scratch_shapes=[pltpu.VMEM((N_EXPERTS, LANES), F32)],
        compiler_params=_cparams("arbitrary"),
        name="moe_route",
    )(logits, b_col, tri)


def _dispatch_kernel(dst_ref, ps_ref, cnt_ref, pe_ref, h_ref, wgu_ref, wd_ref, xs_ref, sh_ref,
                     zero_scr, sem, zsem):
    i = pl.program_id(0)
    tm = h_ref.shape[0] // ROW_TILE

    def tile_at(ref, r):
        return ref.at[pl.ds(pl.multiple_of(r * ROW_TILE, ROW_TILE), ROW_TILE), :]

    @pl.when(i == 0)
    def _():
        _store_zero_rows(zero_scr)

        def per_expert(e, n):
            lo = ps_ref[e] + cnt_ref[e]
            hi = pe_ref[e]

            def issue(rw, c):
                pltpu.make_async_copy(tile_at(zero_scr, 0), tile_at(xs_ref, rw), zsem).start()
                return c
            lax.fori_loop(lo, hi, issue, 0)
            return n + (hi - lo)
        n_pad = lax.fori_loop(0, N_EXPERTS, per_expert, 0)

        def drain(_, c):
            pltpu.make_async_copy(tile_at(zero_scr, 0), tile_at(xs_ref, 0), zsem).wait()
            return c
        lax.fori_loop(0, n_pad, drain, 0)

        blk_rows = zero_scr.shape[0]

        def tail_copy(b):
            return pltpu.make_async_copy(
                zero_scr, xs_ref.at[pl.ds(pl.multiple_of(b * blk_rows, blk_rows), blk_rows), :], zsem)
        first_free = pe_ref[N_EXPERTS - 1] * ROW_TILE // blk_rows
        n_blocks = xs_ref.shape[0] // blk_rows
        lax.fori_loop(first_free, n_blocks, lambda b, c: (tail_copy(b).start(), c)[1], 0)
        lax.fori_loop(first_free, n_blocks, lambda b, c: (tail_copy(b).wait(), c)[1], 0)

    def issue_row(r, c):
        src = tile_at(h_ref, r)
        for k in range(TOP_K):
            pltpu.make_async_copy(src, tile_at(xs_ref, dst_ref[k, r]), sem).start()
        return c
    lax.fori_loop(0, tm, issue_row, 0)

    words = jnp.concatenate([_load_row_chunk(h_ref, c, tm) for c in range(ROW_TILE)], axis=1)
    lo, hi = _unpack_rows(words)
    lo, hi = lo.astype(BF16), hi.astype(BF16)
    half = lo.shape[1]
    gu = _dot(lo, wgu_ref[:half, :]) + _dot(hi, wgu_ref[half:, :])
    ff = gu.shape[1] // 2
    gate, up = gu[:, :ff], gu[:, ff:]
    act = (gate * jax.nn.sigmoid(gate) * up).astype(BF16)
    sh_ref[...] = _dot(act, wd_ref[...]).astype(BF16)

    for k in range(TOP_K):
        pltpu.make_async_copy(h_ref, xs_ref.at[pl.ds(0, tm * ROW_TILE), :], sem).wait()


def _dispatch(dest, pad_start, counts, pad_end, h_packed, w_sh_gu, w_sh_down, n_rows):
    N = h_packed.shape[0] // ROW_TILE
    D = w_sh_gu.shape[0]
    tm = 256
    whole = lambda: pl.BlockSpec(memory_space=pltpu.SMEM)
    return pl.pallas_call(
        _dispatch_kernel,
        grid=(N // tm,),
        in_specs=[pl.BlockSpec((TOP_K, tm), lambda i: (0, i), memory_space=pltpu.SMEM),
                  whole(), whole(), whole(),
                  pl.BlockSpec((tm * ROW_TILE, LANES), lambda i: (i, 0)),
                  pl.BlockSpec(w_sh_gu.shape, lambda i: (0, 0)),
                  pl.BlockSpec(w_sh_down.shape, lambda i: (0, 0))],
        out_specs=[pl.BlockSpec(memory_space=pl.ANY), pl.BlockSpec((tm, D), lambda i: (i, 0))],
        out_shape=[jax.ShapeDtypeStruct((n_rows * ROW_TILE, LANES), U32),
                   jax.ShapeDtypeStruct((N, D), BF16)],
        scratch_shapes=[pltpu.VMEM((MOE_BLK * ROW_TILE, LANES), U32),
                        pltpu.SemaphoreType.DMA, pltpu.SemaphoreType.DMA],
        compiler_params=_cparams("arbitrary"),
        name="moe_dispatch",
    )(dest, pad_start, counts, pad_end, h_packed, w_sh_gu, w_sh_down)


def _expert_kernel(bexp_ref, nused_ref, xs_ref, wg_ref, wu_ref, wd_ref, ys_ref, wg_scr, wu_scr, wd_scr):
    i = pl.program_id(0)
    used = i < nused_ref[0]

    @pl.when(jnp.logical_not(used))
    def _():
        _store_zero_rows(ys_ref)

    @pl.when(used & ((i == 0) | (bexp_ref[i] != bexp_ref[jnp.maximum(i - 1, 0)])))
    def _():
        wg_scr[...] = wg_ref[0].astype(BF16)
        wu_scr[...] = wu_ref[0].astype(BF16)
        wd_scr[...] = wd_ref[0].astype(BF16)

    @pl.when(used)
    def _():
        m = xs_ref.shape[0] // ROW_TILE
        words = jnp.concatenate([_load_row_chunk(xs_ref, c, m) for c in range(ROW_TILE)], axis=1)
        lo, hi = _unpack_rows(words)
        lo, hi = lo.astype(BF16), hi.astype(BF16)
        half = lo.shape[1]
        gate = _dot(lo, wg_scr[:half, :]) + _dot(hi, wg_scr[half:, :])
        up = _dot(lo, wu_scr[:half, :]) + _dot(hi, wu_scr[half:, :])
        act = (gate * jax.nn.sigmoid(gate) * up).astype(BF16)
        _store_rows(ys_ref, _pack_rows(_dot(act, wd_scr[...])))


def _experts(blk_exp, n_used, xs, w_gate, w_up, w_down):
    P = xs.shape[0] // ROW_TILE
    n_blk = P // MOE_BLK
    D, FF = w_gate.shape[1], w_gate.shape[2]
    row_map = lambda i, be, nu: (jnp.minimum(i, nu[0] - 1), 0)
    return pl.pallas_call(
        _expert_kernel,
        grid_spec=pltpu.PrefetchScalarGridSpec(
            num_scalar_prefetch=2,
            grid=(n_blk,),
            in_specs=[pl.BlockSpec((MOE_BLK * ROW_TILE, LANES), row_map),
                      pl.BlockSpec((1, D, FF), lambda i, be, nu: (be[i], 0, 0)),
                      pl.BlockSpec((1, D, FF), lambda i, be, nu: (be[i], 0, 0)),
                      pl.BlockSpec((1, FF, D), lambda i, be, nu: (be[i], 0, 0))],
            out_specs=pl.BlockSpec((MOE_BLK * ROW_TILE, LANES), lambda i, be, nu: (i, 0)),
            scratch_shapes=[pltpu.VMEM((D, FF), BF16), pltpu.VMEM((D, FF), BF16), pltpu.VMEM((FF, D), BF16)]),
        out_shape=jax.ShapeDtypeStruct(xs.shape, U32),
        compiler_params=_cparams("arbitrary"),
        name="moe_experts",
    )(blk_exp, n_used, xs, w_gate, w_up, w_down)


def _combine_kernel(dcur_ref, dnxt_ref, w_ref, sh_ref, x_ref, mod_ref, g_ref, ys_ref, o_ref, buf2, sems):
    i = pl.program_id(0)
    n = pl.num_programs(0)
    tm = x_ref.shape[0]
    half = ROW_TILE * LANES

    def tile_at(ref, r):
        return ref.at[pl.ds(pl.multiple_of(r * ROW_TILE, ROW_TILE), ROW_TILE), :]

    def gather(slots_ref, slot):
        def issue_row(r, c):
            for k in range(TOP_K):
                pltpu.make_async_copy(tile_at(ys_ref, slots_ref[k, r]),
                                      tile_at(buf2.at[slot, k], r), sems.at[slot]).start()
            return c
        lax.fori_loop(0, tm, issue_row, 0)

    @pl.when(i == 0)
    def _():
        gather(dcur_ref, 0)

    @pl.when(i + 1 < n)
    def _():
        gather(dnxt_ref, (i + 1) % 2)

    slot = i % 2
    buf = buf2.at[slot]
    for k in range(TOP_K):
        pltpu.make_async_copy(ys_ref.at[pl.ds(0, tm * ROW_TILE), :], buf.at[k], sems.at[slot]).wait()

    w = w_ref[...]
    ssq = jnp.zeros((tm, 1), F32)
    for c in range(ROW_TILE):
        c_lo = slice(c * LANES, (c + 1) * LANES)
        c_hi = slice(half + c * LANES, half + (c + 1) * LANES)
        acc_lo = sh_ref[:, c_lo].astype(F32)
        acc_hi = sh_ref[:, c_hi].astype(F32)
        for k in range(TOP_K):
            lo, hi = _unpack_rows(_load_row_chunk(buf.at[k], c, tm))
            wk = w[:, k:k + 1]
            acc_lo = acc_lo + wk * lo
            acc_hi = acc_hi + wk * hi
        o_ref[:, c_lo] = acc_lo
        o_ref[:, c_hi] = acc_hi
        ssq = ssq + jnp.sum(acc_lo * acc_lo, axis=-1, keepdims=True) + jnp.sum(acc_hi * acc_hi, axis=-1, keepdims=True)
    inv = lax.rsqrt(ssq / (2 * half) + EPS)
    o_ref[...] = x_ref[...] + mod_ref[0, 5:6, :] * (o_ref[...] * inv * g_ref[...])


def _combine(dest, w_tok, shared, x1, mod3, g, ys, T):
    N, D = x1.shape
    tm = 256
    per_b = T // tm
    last = N // tm - 1
    return pl.pallas_call(
        _combine_kernel,
        grid=(N // tm,),
        in_specs=[pl.BlockSpec((TOP_K, tm), lambda i: (0, i), memory_space=pltpu.SMEM),
                  pl.BlockSpec((TOP_K, tm), lambda i: (0, jnp.minimum(i + 1, last)), memory_space=pltpu.SMEM),
                  pl.BlockSpec((tm, TOP_K), lambda i: (i, 0)),
                  pl.BlockSpec((tm, D), lambda i: (i, 0)),
                  pl.BlockSpec((tm, D), lambda i: (i, 0)),
                  pl.BlockSpec((1, 6, D), lambda i: (i // per_b, 0, 0)),
                  pl.BlockSpec((1, D), lambda i: (0, 0)),
                  pl.BlockSpec(memory_space=pl.ANY)],
        out_specs=pl.BlockSpec((tm, D), lambda i: (i, 0)),
        out_shape=jax.ShapeDtypeStruct((N, D), F32),
        scratch_shapes=[pltpu.VMEM((2, TOP_K, tm * ROW_TILE, LANES), U32), pltpu.SemaphoreType.DMA((2,))],
        compiler_params=_cparams("arbitrary"),
        name="moe_combine",
    )(dest, dest, w_tok, shared, x1, mod3, g.reshape(1, D), ys)


def _inproj_columns():
    starts = np.concatenate([[0], np.cumsum(IN_SIZES)])
    s_q, s_kv, s_ga, s_qkm, s_vm, s_if, s_om, s_a, s_b = starts[:9]
    cols = np.full((PROJ_W,), D_IN, np.int64)
    cols[OFF_Q:OFF_Q + NSA_Q] = s_q + np.arange(NSA_Q)
    cols[OFF_GATE_A:OFF_GATE_A + D_MODEL] = s_a + np.arange(D_MODEL)
    cols[OFF_GATE_B:OFF_GATE_B + D_MODEL] = s_b + np.arange(D_MODEL)
    for br in range(3):
        for g in range(NSA_KV_GROUPS):
            for kv in range(2):
                dst = OFF_KV + (br * NSA_KV_GROUPS + g) * LANES + kv * NSA_DH
                src = s_kv + (2 * br + kv) * NSA_KV + g * NSA_DH
                cols[dst:dst + NSA_DH] = src + np.arange(NSA_DH)
    cols[OFF_OM:OFF_OM + M_V] = s_om + np.arange(M_V)
    cols[OFF_VM:OFF_VM + M_V] = s_vm + np.arange(M_V)
    cols[OFF_QKM:OFF_QKM + 2 * M_QK] = s_qkm + np.arange(2 * M_QK)
    per_g = 3 * NSA_HPG
    for g in range(NSA_KV_GROUPS):
        cols[OFF_GA + g * LANES:OFF_GA + g * LANES + per_g] = s_ga + g * per_g + np.arange(per_g)
    cols_if = np.full((LANES,), D_IN, np.int64)
    cols_if[:2 * M_HEADS] = s_if + np.arange(2 * M_HEADS)
    return cols, cols_if


def _take_columns(w, cols):
    pieces = []
    start = 0
    for i in range(1, len(cols) + 1):
        run_ends = (i == len(cols) or (cols[start] == D_IN) != (cols[i] == D_IN)
                    or (cols[start] != D_IN and cols[i] != cols[i - 1] + 1))
        if run_ends:
            if cols[start] == D_IN:
                pieces.append(jnp.zeros((w.shape[0], i - start), w.dtype))
            else:
                pieces.append(w[:, int(cols[start]):int(cols[start]) + i - start])
            start = i
    return jnp.concatenate(pieces, axis=1)


def _cmp_to_sel(n_cmp, n_sel):
    ci = np.arange(n_cmp)[:, None]
    sj = np.arange(n_sel)[None, :]
    overlap = (np.minimum(ci * CMP_STRIDE + CMP_LEN, (sj + 1) * SEL_LEN)
               - np.maximum(ci * CMP_STRIDE, sj * SEL_LEN))
    m = np.zeros((LANES, LANES), np.float32)
    m[:n_cmp, :n_sel] = np.clip(overlap, 0, None).astype(np.float32) / CMP_LEN
    return m


def _layer(x, mod3, g_pre_mix, g_post_mix, w_in, cmp_pe, cmp_w1, cmp_w2, conv_w, conv_b, b_gates_m,
           mh_norm_g, w_up_nsa, w_up_mlstm, w_out, g_pre_ffn, g_post_ffn, w_router, b_router,
           w_e_gate, w_e_up, w_e_down, w_sh_gate, w_sh_up, w_sh_down):
    B, T, D = x.shape
    N = B * T
    x2 = x.reshape(N, D)

    cols, cols_if = _inproj_columns()
    w_in_b = w_in.astype(BF16)
    proj, gates = _inproj(x2, mod3, g_pre_mix, _take_columns(w_in_b, cols), _take_columns(w_in_b, cols_if), T)
    proj3 = proj.reshape(B, T, PROJ_W)

    G = NSA_KV_GROUPS
    nb = T // CMP_STRIDE
    kvc_in = proj3[:, :, OFF_KV:OFF_KV + G * LANES].reshape(B, nb, CMP_STRIDE, G, 2, NSA_DH)
    kvc_in = kvc_in.transpose(4, 0, 3, 1, 2, 5).reshape(2, B, G, nb, CMP_STRIDE * NSA_DH)
    pe2 = cmp_pe.reshape(2, 2, CMP_STRIDE * NSA_DH)
    w2x = jnp.stack([jnp.pad(cmp_w2[0], ((0, 0), (0, NSA_DH))),
                     jnp.pad(cmp_w2[1], ((0, 0), (NSA_DH, 0)))]).astype(BF16)
    kvc = _compress(kvc_in[0], kvc_in[1], pe2, cmp_w1.astype(BF16), w2x)

    n_sel = T // SEL_LEN
    o_nsa = _nsa_attention(proj3, kvc, jnp.asarray(_cmp_to_sel(nb - 1, n_sel).T))

    L = min(MLSTM_CHUNK, T)
    gates3 = gates.reshape(B, T, LANES)
    gates_row = gates3[:, :, :2 * M_HEADS].reshape(B, T // L, L, 2 * M_HEADS).transpose(0, 1, 3, 2)
    o_ml = _mlstm(proj3, gates3, gates_row, conv_w, conv_b, b_gates_m, mh_norm_g)

    merged = _merge(o_nsa.reshape(N, NSA_Q), o_ml.reshape(N, M_V), proj,
                    w_up_nsa.astype(BF16), w_up_mlstm.astype(BF16))
    x1 = _outproj(merged, w_out.astype(BF16), x2, mod3, g_post_mix, T)

    w_router_pad = jnp.pad(w_router, ((0, 0), (0, LANES - N_EXPERTS)))
    w_sh_gu = jnp.concatenate([w_sh_gate, w_sh_up], axis=1).astype(BF16)
    h_packed, logits = _ffn_in(x1, mod3, g_pre_ffn, w_router_pad, T)
    eidx, rnk, w_k, counts2 = _route(logits, b_router)

    counts = counts2[:, 0]
    padded = (counts + MOE_BLK - 1) // MOE_BLK * MOE_BLK
    pad_end = jnp.cumsum(padded).astype(I32)
    pad_start = pad_end - padded
    n_blk = (N * TOP_K) // MOE_BLK + N_EXPERTS
    blk_first = jnp.arange(n_blk, dtype=I32) * MOE_BLK
    blk_exp = jnp.minimum(jnp.sum((pad_end[None, :] <= blk_first[:, None]).astype(I32), axis=1), N_EXPERTS - 1)
    n_used = (pad_end[-1:] // MOE_BLK).astype(I32)
    experts = jnp.arange(N_EXPERTS, dtype=I32)
    dest = rnk + jnp.sum(jnp.where(eidx[:, :, None] == experts, pad_start, 0), axis=-1)

    xs, shared = _dispatch(dest, pad_start, counts, pad_end, h_packed, w_sh_gu, w_sh_down.astype(BF16),
                           n_blk * MOE_BLK)
    ys = _experts(blk_exp, n_used, xs, w_e_gate, w_e_up, w_e_down)
    out = _combine(dest, jnp.transpose(w_k), shared, x1, mod3, g_post_ffn, ys, T)
    return out.reshape(B, T, D)


def kernel(x, c, w_ada, b_ada, g_pre_mix, g_post_mix, w_in, cmp_pe, cmp_w1, cmp_w2, conv_w, conv_b, b_gates_m, mh_norm_g, w_up_nsa, w_up_mlstm, w_out, g_pre_ffn, g_post_ffn, w_router, b_router, w_e_gate, w_e_up, w_e_down, w_sh_gate, w_sh_up, w_sh_down):
    B = x.shape[0]
    for l in range(w_ada.shape[0]):
        mod3 = _ada(c, w_ada[l], b_ada[l]).reshape(B, 6, D_MODEL)
        x = _layer(x, mod3, g_pre_mix[l], g_post_mix[l], w_in[l], cmp_pe[l], cmp_w1[l], cmp_w2[l],
                   conv_w[l], conv_b[l], b_gates_m[l], mh_norm_g[l], w_up_nsa[l], w_up_mlstm[l], w_out[l],
                   g_pre_ffn[l], g_post_ffn[l], w_router[l], b_router[l], w_e_gate[l], w_e_up[l],
                   w_e_down[l], w_sh_gate[l], w_sh_up[l], w_sh_down[l])
    return x
```

```python
import functools

import numpy as np
import jax
import jax.numpy as jnp
from jax import lax
from jax.experimental import pallas as pl
from jax.experimental.pallas import tpu as pltpu

F32 = jnp.float32
BF16 = jnp.bfloat16
I32 = jnp.int32
U32 = jnp.uint32

D_MODEL = 2048
NSA_HEADS = 16
NSA_KV_GROUPS = 4
NSA_HPG = NSA_HEADS // NSA_KV_GROUPS
NSA_DH = 64
NSA_Q = NSA_HEADS * NSA_DH
NSA_KV = NSA_KV_GROUPS * NSA_DH
CMP_STRIDE = 16
CMP_LEN = 2 * CMP_STRIDE
CMP_HID = 256
SEL_LEN = 64
SEL_TOP = 16
WIN = 512
FORCE = 1e3
NEG = -1e9
M_HEADS = 4
M_DQK = 128
M_DV = 256
M_QK = M_HEADS * M_DQK
M_V = M_HEADS * M_DV
CONV_W = 4
N_EXPERTS = 64
TOP_K = 8
N_GROUPS = 8
GROUP_SIZE = N_EXPERTS // N_GROUPS
TOP_GROUPS = 4
EXPERT_FF = 512
SHARED_FF = 512
ROUTE_SCALE = 2.5
EPS = 1e-6

IN_SIZES = (NSA_Q, 6 * NSA_KV, 3 * NSA_HEADS, 2 * M_QK, M_V, 2 * M_HEADS, M_V, D_MODEL, D_MODEL)
D_IN = sum(IN_SIZES)

LANES = 128
VMEM_LIMIT = 56 * 1024 * 1024

OFF_Q = 0
OFF_KV = 1024
OFF_OM = 2560
OFF_GA = 3584
OFF_GATE_A = 4096
OFF_GATE_B = 6144
OFF_VM = 8192
OFF_QKM = 9216
PROJ_W = 10240

MLSTM_CHUNK = 256
ATT_TILE = 256
MOE_BLK = 256
ROW_TILE = 8


def _cparams(*sem):
    return pltpu.CompilerParams(dimension_semantics=sem, vmem_limit_bytes=VMEM_LIMIT)


def _dot(a, b, **kw):
    return jnp.dot(a, b, preferred_element_type=F32, **kw)


def _dot_nt(a, b):
    return lax.dot_general(a, b, (((1,), (1,)), ((), ())), preferred_element_type=F32)


def _ada_kernel(c_ref, w_ref, b_ref, o_ref):
    c = c_ref[...]
    sc = (c * jax.nn.sigmoid(c)).astype(BF16)
    o_ref[...] = _dot(sc, w_ref[...].astype(BF16)) + b_ref[...]


def _ada(c, w_ada, b_ada):
    B, D = c.shape
    n_out = w_ada.shape[1]
    tn = 1024
    return pl.pallas_call(
        _ada_kernel,
        grid=(n_out // tn,),
        in_specs=[pl.BlockSpec((B, D), lambda j: (0, 0)),
                  pl.BlockSpec((D, tn), lambda j: (0, j)),
                  pl.BlockSpec((1, tn), lambda j: (0, j))],
        out_specs=pl.BlockSpec((B, tn), lambda j: (0, j)),
        out_shape=jax.ShapeDtypeStruct((B, n_out), F32),
        compiler_params=_cparams("arbitrary"),
        name="ada_mod",
    )(c, w_ada, b_ada.reshape(1, n_out))


def _norm_mod(x, g, shift, scale):
    ms = jnp.mean(x * x, axis=-1, keepdims=True)
    y = x * lax.rsqrt(ms + EPS) * g
    return y * (1.0 + scale) + shift


def _inproj_kernel(x_ref, mod_ref, g_ref, w_ref, wif_ref, o_ref, oif_ref, h_scr):
    @pl.when(pl.program_id(1) == 0)
    def _():
        h = _norm_mod(x_ref[...], g_ref[...], mod_ref[0, 0:1, :], mod_ref[0, 1:2, :])
        hb = h.astype(BF16)
        h_scr[...] = hb
        oif_ref[...] = _dot(hb, wif_ref[...])

    o_ref[...] = _dot(h_scr[...], w_ref[...]).astype(BF16)


def _inproj(x2, mod3, g, w_main, w_if, T):
    N, D = x2.shape
    tm, tn = min(1024, T), 1024
    per_b = T // tm
    return pl.pallas_call(
        _inproj_kernel,
        grid=(N // tm, PROJ_W // tn),
        in_specs=[pl.BlockSpec((tm, D), lambda i, j: (i, 0)),
                  pl.BlockSpec((1, 6, D), lambda i, j: (i // per_b, 0, 0)),
                  pl.BlockSpec((1, D), lambda i, j: (0, 0)),
                  pl.BlockSpec((D, tn), lambda i, j: (0, j)),
                  pl.BlockSpec((D, LANES), lambda i, j: (0, 0))],
        out_specs=[pl.BlockSpec((tm, tn), lambda i, j: (i, j)),
                   pl.BlockSpec((tm, LANES), lambda i, j: (i, 0))],
        out_shape=[jax.ShapeDtypeStruct((N, PROJ_W), BF16),
                   jax.ShapeDtypeStruct((N, LANES), F32)],
        scratch_shapes=[pltpu.VMEM((tm, D), BF16)],
        compiler_params=_cparams("arbitrary", "arbitrary"),
        name="in_proj",
    )(x2, mod3, g.reshape(1, D), w_main, w_if)


def _compress_kernel(sk_ref, sv_ref, pe_ref, w1_ref, w2_ref, o_ref):
    nb = sk_ref.shape[2]
    half = CMP_STRIDE * NSA_DH
    out = None
    for kv, s_ref in enumerate((sk_ref, sv_ref)):
        s = s_ref[0, 0].astype(F32)
        top = (s + pe_ref[kv, 0:1, :]).astype(BF16)
        bot = (s + pe_ref[kv, 1:2, :]).astype(BF16)
        a = _dot(top, w1_ref[kv, :half, :])
        b = _dot(bot, w1_ref[kv, half:, :])
        hid = jax.nn.gelu(a + pltpu.roll(b, nb - 1, axis=0))
        y = _dot(hid.astype(BF16), w2_ref[kv])
        out = y if out is None else out + y
    o_ref[0, 0] = out.astype(BF16)


def _compress(sk, sv, pe2, w1, w2x):
    B, G, nb, width = sk.shape
    blk = pl.BlockSpec((1, 1, nb, width), lambda b, g: (b, g, 0, 0))
    return pl.pallas_call(
        _compress_kernel,
        grid=(B, G),
        in_specs=[blk, blk,
                  pl.BlockSpec(pe2.shape, lambda b, g: (0, 0, 0)),
                  pl.BlockSpec(w1.shape, lambda b, g: (0, 0, 0)),
                  pl.BlockSpec(w2x.shape, lambda b, g: (0, 0, 0))],
        out_specs=pl.BlockSpec((1, 1, nb, LANES), lambda b, g: (b, g, 0, 0)),
        out_shape=jax.ShapeDtypeStruct((B, G, nb, LANES), BF16),
        compiler_params=_cparams("arbitrary", "arbitrary"),
        name="nsa_compress",
    )(sk, sv, pe2, w1, w2x)


def _nsa_kernel(q_ref, kvc_ref, kvs_ref, kvw_ref, ga_ref, c2st_ref, o_ref,
                kx_scr, tri_scr, m_scr, l_scr, acc_scr, *, n_cmp, n_sel, top):
    qi = pl.program_id(2)
    tq = q_ref.shape[1]
    tk = tq
    T = kvs_ref.shape[1]
    H = NSA_HPG
    scale = NSA_DH ** -0.5
    n_sel_pad = -(-n_sel // 8) * 8

    @pl.when(qi == 0)
    def _():
        rt = lax.broadcasted_iota(I32, (T, LANES), 0)
        lt = lax.broadcasted_iota(I32, (T, LANES), 1)
        onehot = jnp.where(lt - NSA_DH == rt // SEL_LEN, 1.0, 0.0)
        kx_scr[...] = jnp.where(lt < NSA_DH, kvs_ref[0].astype(F32), onehot).astype(BF16)
        r2 = lax.broadcasted_iota(I32, (H * tq, tk), 0) % tq
        c2 = lax.broadcasted_iota(I32, (H * tq, tk), 1)
        tri_scr[0] = jnp.where(c2 <= r2, 0.0, NEG)
        tri_scr[1] = jnp.where(c2 > r2, 0.0, NEG)

    row = lax.broadcasted_iota(I32, (tq, LANES), 0)
    lane = lax.broadcasted_iota(I32, (tq, LANES), 1)
    t_abs = qi * tq + row
    lo = lane < NSA_DH

    q2 = q_ref[0].astype(F32) * scale
    qh = []
    for pr in range(H // 2):
        qp = q2[:, pr * LANES:(pr + 1) * LANES]
        qh.append(qp)
        qh.append(pltpu.roll(qp, NSA_DH, axis=1))
    qz = jnp.concatenate([jnp.where(lo, x, 0.0) for x in qh], axis=0).astype(BF16)

    kvc = kvc_ref[0, 0]
    cmp_ok = (lane < n_cmp) & (lane * CMP_STRIDE + (CMP_LEN - 1) <= t_abs)
    cmp_ok = jnp.concatenate([cmp_ok] * H, axis=0)
    s = jnp.where(cmp_ok, _dot_nt(qz, kvc), NEG)
    m = jnp.max(s, axis=-1, keepdims=True)
    p = jnp.where(cmp_ok, jnp.exp(s - m), 0.0)
    l = jnp.sum(p, axis=-1, keepdims=True)
    p = p / jnp.where(l > 0.0, l, 1.0)
    acc_scr[2] = _dot(p.astype(BF16), kvc)
    psum = p[0:tq]
    for h in range(1, H):
        psum = psum + p[h * tq:(h + 1) * tq]

    imp_t = lax.dot_general(c2st_ref[...], psum, (((1,), (1,)), ((), ())),
                            preferred_element_type=F32, precision=lax.Precision.HIGHEST)[:n_sel_pad, :]
    blk = lax.broadcasted_iota(I32, (n_sel_pad, tq), 0)
    cur = (qi * tq + lax.broadcasted_iota(I32, (n_sel_pad, tq), 1)) // SEL_LEN
    valid = (blk <= cur) & (blk < n_sel)
    forced = (blk == 0) | (blk == cur) | (blk == cur - 1)
    score = jnp.where(valid, imp_t + jnp.where(forced, FORCE, 0.0), NEG)
    rank = jnp.zeros((n_sel_pad, tq), I32)
    for j in range(n_sel):
        rowv = score[j:j + 1, :]
        rank = rank + ((rowv > score) | ((rowv == score) & (blk > j))).astype(I32)
    nsel_t = jnp.where(valid & (rank < top), 0.0, NEG)
    nsel_t = jnp.concatenate([nsel_t, jnp.zeros((LANES - n_sel_pad, tq), F32)], axis=0)
    nsel = pltpu.roll(jnp.transpose(nsel_t), NSA_DH, axis=1)
    qs = jnp.concatenate([jnp.where(lo, x, nsel) for x in qh], axis=0).astype(BF16)

    m_scr[...] = jnp.full(m_scr.shape, -1e30, F32)
    l_scr[...] = jnp.zeros(l_scr.shape, F32)
    acc_scr[0:2] = jnp.zeros((2, H * tq, LANES), F32)

    def step(br, queries, k_tile, v_tile, bias):
        s = _dot_nt(queries, k_tile)
        if bias is not None:
            s = s + bias
        m_prev = m_scr[br]
        m_new = jnp.maximum(m_prev, jnp.max(s, axis=-1, keepdims=True))
        alpha = jnp.exp(m_prev - m_new)
        p = jnp.exp(s - jnp.concatenate([m_new] * (tk // LANES), axis=1))
        l_scr[br] = alpha * l_scr[br] + jnp.sum(p, axis=-1, keepdims=True)
        acc_scr[br] = alpha * acc_scr[br] + _dot(p.astype(BF16), v_tile)
        m_scr[br] = m_new

    def sel_tile(kt, carry):
        off = pl.multiple_of(kt * tk, tk)
        step(0, qs, kx_scr[pl.ds(off, tk), :], kvs_ref[0, pl.ds(off, tk), :], None)
        return carry
    lax.fori_loop(0, qi, sel_tile, 0)

    def win_tile(back, bias):
        off = pl.multiple_of((qi - back) * tk, tk)
        w = kvw_ref[0, pl.ds(off, tk), :]
        step(1, qz, w, w, bias)

    @pl.when(qi >= 2)
    def _():
        win_tile(2, tri_scr[1])

    @pl.when(qi >= 1)
    def _():
        win_tile(1, None)

    diag = pl.multiple_of(qi * tk, tk)
    step(0, qs, kx_scr[pl.ds(diag, tk), :], kvs_ref[0, pl.ds(diag, tk), :], tri_scr[0])
    win_tile(0, tri_scr[0])

    gates = jax.nn.sigmoid(ga_ref[0].astype(F32))

    def gate_rows(br):
        return jnp.concatenate([jnp.broadcast_to(gates[:, 3 * h + br:3 * h + br + 1], (tq, LANES))
                                for h in range(H)], axis=0)
    o_all = (gate_rows(0) * acc_scr[2] + gate_rows(1) * (acc_scr[0] / l_scr[0])
             + gate_rows(2) * (acc_scr[1] / l_scr[1]))
    for pr in range(H // 2):
        even = o_all[2 * pr * tq:(2 * pr + 1) * tq]
        odd = o_all[(2 * pr + 1) * tq:(2 * pr + 2) * tq]
        pair = jnp.where(lo, pltpu.roll(even, NSA_DH, axis=1), odd)
        o_ref[0, :, pr * LANES:(pr + 1) * LANES] = pair.astype(BF16)


def _nsa_attention(proj3, kvc, c2s_t):
    B, T, _ = proj3.shape
    G = NSA_KV_GROUPS
    tq = min(ATT_TILE, T)
    nq = T // tq
    n_cmp = T // CMP_STRIDE - 1
    n_sel = T // SEL_LEN
    top = min(SEL_TOP, n_sel)
    gw = NSA_HPG * NSA_DH
    assert kvc.shape[2] <= LANES and n_sel <= NSA_DH and WIN == 2 * tq
    kvc = jnp.pad(kvc, ((0, 0), (0, 0), (0, LANES - kvc.shape[2]), (0, 0)))
    kern = functools.partial(_nsa_kernel, n_cmp=n_cmp, n_sel=n_sel, top=top)
    return pl.pallas_call(
        kern,
        grid=(B, G, nq),
        in_specs=[
            pl.BlockSpec((1, tq, gw), lambda b, g, i: (b, i, OFF_Q // gw + g)),
            pl.BlockSpec((1, 1, LANES, LANES), lambda b, g, i: (b, g, 0, 0)),
            pl.BlockSpec((1, T, LANES), lambda b, g, i: (b, 0, OFF_KV // LANES + G + g)),
            pl.BlockSpec((1, T, LANES), lambda b, g, i: (b, 0, OFF_KV // LANES + 2 * G + g)),
            pl.BlockSpec((1, tq, LANES), lambda b, g, i: (b, i, OFF_GA // LANES + g)),
            pl.BlockSpec((LANES, LANES), lambda b, g, i: (0, 0)),
        ],
        out_specs=pl.BlockSpec((1, tq, gw), lambda b, g, i: (b, i, g)),
        out_shape=jax.ShapeDtypeStruct((B, T, NSA_Q), BF16),
        scratch_shapes=[pltpu.VMEM((T, LANES), BF16),
                        pltpu.VMEM((2, NSA_HPG * tq, tq), F32),
                        pltpu.VMEM((2, NSA_HPG * tq, LANES), F32),
                        pltpu.VMEM((2, NSA_HPG * tq, LANES), F32),
                        pltpu.VMEM((3, NSA_HPG * tq, LANES), F32)],
        compiler_params=_cparams("arbitrary", "arbitrary", "arbitrary"),
        name="nsa_attention",
    )(proj3, kvc, proj3, proj3, proj3, c2s_t)


def _log_sigmoid(x):
    return -(jnp.maximum(-x, 0.0) + jnp.log1p(jnp.exp(-jnp.abs(x))))


def _mlstm_kernel(q_ref, k_ref, v_ref, o_ref, gc_ref, gr_ref, cwq_ref, cwk_ref, cbq_ref, cbk_ref,
                  bg_ref, ng_ref, out_ref, q_scr, k_scr, c_scr):
    hd = pl.program_id(1)
    T = q_ref.shape[1]
    L = gr_ref.shape[3]
    nc = T // L
    dv = v_ref.shape[2]

    trow = lax.broadcasted_iota(I32, (T, M_DQK), 0)

    def conv_silu(x_ref, w_ref, b_ref):
        x = x_ref[0].astype(F32)
        y = b_ref[...] + w_ref[CONV_W - 1:CONV_W, :] * x
        for d in range(1, CONV_W):
            xs = jnp.where(trow >= d, pltpu.roll(x, d, axis=0), 0.0)
            y = y + w_ref[CONV_W - 1 - d:CONV_W - d, :] * xs
        return y * jax.nn.sigmoid(y)

    q_scr[...] = conv_silu(q_ref, cwq_ref, cbq_ref).astype(BF16)
    k_scr[...] = (conv_silu(k_ref, cwk_ref, cbk_ref) * (M_DQK ** -0.5)).astype(BF16)
    c_scr[...] = jnp.zeros_like(c_scr)

    lane_l = lax.broadcasted_iota(I32, (L, LANES), 1)
    sub8 = lax.broadcasted_iota(I32, (2 * M_HEADS, L), 0)
    r_i = lax.broadcasted_iota(I32, (L, L), 0)
    c_i = lax.broadcasted_iota(I32, (L, L), 1)
    tri = c_i <= r_i
    ones_blk = jnp.where(lane_l == 0, 1.0, 0.0).astype(BF16)

    bl = lax.broadcasted_iota(I32, (1, LANES), 1)
    bg = bg_ref[...]
    b_i = jnp.sum(jnp.where(bl == hd, bg, 0.0), axis=-1, keepdims=True)
    b_f = jnp.sum(jnp.where(bl == hd + M_HEADS, bg, 0.0), axis=-1, keepdims=True)

    def chunk(c, m):
        off = pl.multiple_of(c * L, L)
        gc = gc_ref[0, pl.ds(off, L), :]
        gr = gr_ref[0, c]
        li_col = jnp.sum(jnp.where(lane_l == hd, gc, 0.0), axis=-1, keepdims=True) + b_i
        lf_col = _log_sigmoid(jnp.sum(jnp.where(lane_l == hd + M_HEADS, gc, 0.0), axis=-1, keepdims=True) + b_f)
        li_row = jnp.sum(jnp.where(sub8 == hd, gr, 0.0), axis=0, keepdims=True) + b_i
        lf_row = _log_sigmoid(jnp.sum(jnp.where(sub8 == hd + M_HEADS, gr, 0.0), axis=0, keepdims=True) + b_f)
        a_col = jnp.sum(jnp.where(tri, lf_row, 0.0), axis=-1, keepdims=True)
        a_row = jnp.sum(jnp.where(r_i <= c_i, lf_col, 0.0), axis=0, keepdims=True)
        a_end = jnp.sum(lf_row, axis=-1, keepdims=True)

        dlog = jnp.where(tri, a_col - a_row + li_row, -jnp.inf)
        inter = a_col + m
        mt = jnp.maximum(inter, jnp.max(dlog, axis=-1, keepdims=True))
        dmat = jnp.exp(dlog - mt)
        iw = jnp.exp(inter - mt)

        qc = q_scr[pl.ds(off, L), :]
        kc = k_scr[pl.ds(off, L), :]
        v_aug = jnp.concatenate([v_ref[0, pl.ds(off, L), :], ones_blk], axis=-1)
        s = (_dot_nt(qc, kc) * dmat).astype(BF16)
        cmat = c_scr[...]
        num = iw * _dot(qc, cmat.astype(BF16)) + _dot(s, v_aug)
        den = num[:, dv:dv + 1]
        hc = num[:, :dv] / jnp.maximum(jnp.abs(den), jnp.exp(-mt))

        elog = a_end - a_col + li_col
        m_new = jnp.maximum(a_end + m, jnp.max(elog, axis=0, keepdims=True))
        ew = jnp.exp(elog - m_new)
        decay = jnp.exp(a_end + m - m_new)
        kct = jnp.transpose(kc.astype(F32)).astype(BF16)
        c_scr[...] = decay * cmat + _dot(kct, (ew * v_aug.astype(F32)).astype(BF16))

        hn = hc * lax.rsqrt(jnp.mean(hc * hc, axis=-1, keepdims=True) + EPS) * ng_ref[...]
        og = jax.nn.sigmoid(o_ref[0, pl.ds(off, L), :].astype(F32))
        out_ref[0, pl.ds(off, L), :] = (og * hn).astype(BF16)
        return m_new

    lax.fori_loop(0, nc, chunk, jnp.zeros((1, 1), F32))


def _mlstm(proj3, gates_col, gates_row, conv_w, conv_b, b_gates, norm_g):
    B, T, _ = proj3.shape
    L = gates_row.shape[3]
    nc = T // L
    H = M_HEADS
    qb = OFF_QKM // M_DQK
    return pl.pallas_call(
        _mlstm_kernel,
        grid=(B, H),
        in_specs=[
            pl.BlockSpec((1, T, M_DQK), lambda b, h: (b, 0, qb + h)),
            pl.BlockSpec((1, T, M_DQK), lambda b, h: (b, 0, qb + H + h)),
            pl.BlockSpec((1, T, M_DV), lambda b, h: (b, 0, OFF_VM // M_DV + h)),
            pl.BlockSpec((1, T, M_DV), lambda b, h: (b, 0, OFF_OM // M_DV + h)),
            pl.BlockSpec((1, T, LANES), lambda b, h: (b, 0, 0)),
            pl.BlockSpec((1, nc, 2 * H, L), lambda b, h: (b, 0, 0, 0)),
            pl.BlockSpec((CONV_W, M_DQK), lambda b, h: (0, h)),
            pl.BlockSpec((CONV_W, M_DQK), lambda b, h: (0, H + h)),
            pl.BlockSpec((1, M_DQK), lambda b, h: (0, h)),
            pl.BlockSpec((1, M_DQK), lambda b, h: (0, H + h)),
            pl.BlockSpec((1, LANES), lambda b, h: (0, 0)),
            pl.BlockSpec((1, M_DV), lambda b, h: (0, h)),
        ],
        out_specs=pl.BlockSpec((1, T, M_DV), lambda b, h: (b, 0, h)),
        out_shape=jax.ShapeDtypeStruct((B, T, M_V), BF16),
        scratch_shapes=[pltpu.VMEM((T, M_DQK), BF16), pltpu.VMEM((T, M_DQK), BF16),
                        pltpu.VMEM((M_DQK, M_DV + LANES), F32)],
        compiler_params=_cparams("arbitrary", "arbitrary"),
        name="mlstm",
    )(proj3, proj3, proj3, proj3, gates_col, gates_row, conv_w, conv_w,
      conv_b.reshape(1, -1), conv_b.reshape(1, -1),
      jnp.pad(b_gates, (0, LANES - 2 * H)).reshape(1, LANES), norm_g.reshape(1, -1))


def _merge_kernel(on_ref, om_ref, ga_ref, gb_ref, wn_ref, wm_ref, o_ref):
    a = _dot(on_ref[...], wn_ref[...])
    b = _dot(om_ref[...], wm_ref[...])
    ga = jax.nn.sigmoid(ga_ref[...].astype(F32))
    gb = jax.nn.sigmoid(gb_ref[...].astype(F32))
    o_ref[...] = (ga * a + gb * b).astype(BF16)


def _merge(o_nsa, o_ml, proj, w_up_nsa, w_up_ml):
    N = o_nsa.shape[0]
    D = D_MODEL
    tm = 512
    return pl.pallas_call(
        _merge_kernel,
        grid=(N // tm,),
        in_specs=[pl.BlockSpec((tm, NSA_Q), lambda i: (i, 0)),
                  pl.BlockSpec((tm, M_V), lambda i: (i, 0)),
                  pl.BlockSpec((tm, D), lambda i: (i, OFF_GATE_A // D)),
                  pl.BlockSpec((tm, D), lambda i: (i, OFF_GATE_B // D)),
                  pl.BlockSpec((NSA_Q, D), lambda i: (0, 0)),
                  pl.BlockSpec((M_V, D), lambda i: (0, 0))],
        out_specs=pl.BlockSpec((tm, D), lambda i: (i, 0)),
        out_shape=jax.ShapeDtypeStruct((N, D), BF16),
        compiler_params=_cparams("arbitrary"),
        name="mixer_merge",
    )(o_nsa, o_ml, proj, proj, w_up_nsa, w_up_ml)


def _outproj_kernel(m_ref, w_ref, x_ref, mod_ref, g_ref, o_ref):
    y = _dot(m_ref[...], w_ref[...])
    ms = jnp.mean(y * y, axis=-1, keepdims=True)
    o_ref[...] = x_ref[...] + mod_ref[0, 2:3, :] * (y * lax.rsqrt(ms + EPS) * g_ref[...])


def _outproj(merged, w_out, x2, mod3, g, T):
    N, D = x2.shape
    tm = 512
    per_b = T // tm
    return pl.pallas_call(
        _outproj_kernel,
        grid=(N // tm,),
        in_specs=[pl.BlockSpec((tm, D), lambda i: (i, 0)),
                  pl.BlockSpec((D, D), lambda i: (0, 0)),
                  pl.BlockSpec((tm, D), lambda i: (i, 0)),
                  pl.BlockSpec((1, 6, D), lambda i: (i // per_b, 0, 0)),
                  pl.BlockSpec((1, D), lambda i: (0, 0))],
        out_specs=pl.BlockSpec((tm, D), lambda i: (i, 0)),
        out_shape=jax.ShapeDtypeStruct((N, D), F32),
        compiler_params=_cparams("arbitrary"),
        name="out_proj",
    )(merged, w_out, x2, mod3, g.reshape(1, D))


def _pack_rows(y):
    half = y.shape[1] // 2
    return pltpu.pack_elementwise([y[:, :half], y[:, half:]], packed_dtype=BF16)


def _unpack_rows(w):
    lo = pltpu.unpack_elementwise(w, index=0, packed_dtype=BF16, unpacked_dtype=F32)
    hi = pltpu.unpack_elementwise(w, index=1, packed_dtype=BF16, unpacked_dtype=F32)
    return lo, hi


def _store_rows(ref, packed):
    m = packed.shape[0]
    for c in range(ROW_TILE):
        ref[pl.ds(c, m, stride=ROW_TILE), :] = packed[:, c * LANES:(c + 1) * LANES]


def _store_zero_rows(ref):
    zero = jnp.zeros(ref.shape, F32)
    ref[...] = pltpu.pack_elementwise([zero, zero], packed_dtype=BF16)


def _load_row_chunk(ref, c, m):
    return ref[pl.ds(c, m, stride=ROW_TILE), :]


def _ffn_in_kernel(x_ref, mod_ref, g_ref, wr_ref, hp_ref, lg_ref):
    h = _norm_mod(x_ref[...], g_ref[...], mod_ref[0, 3:4, :], mod_ref[0, 4:5, :])
    lg_ref[...] = _dot(h, wr_ref[...], precision=lax.Precision.HIGHEST)
    _store_rows(hp_ref, _pack_rows(h))


def _ffn_in(x1, mod3, g, w_router_pad, T):
    N, D = x1.shape
    assert D // 2 == ROW_TILE * LANES
    tm = 512
    per_b = T // tm
    return pl.pallas_call(
        _ffn_in_kernel,
        grid=(N // tm,),
        in_specs=[pl.BlockSpec((tm, D), lambda i: (i, 0)),
                  pl.BlockSpec((1, 6, D), lambda i: (i // per_b, 0, 0)),
                  pl.BlockSpec((1, D), lambda i: (0, 0)),
                  pl.BlockSpec((D, LANES), lambda i: (0, 0))],
        out_specs=[pl.BlockSpec((tm * ROW_TILE, LANES), lambda i: (i, 0)),
                   pl.BlockSpec((tm, LANES), lambda i: (i, 0))],
        out_shape=[jax.ShapeDtypeStruct((N * ROW_TILE, LANES), U32),
                   jax.ShapeDtypeStruct((N, LANES), F32)],
        compiler_params=_cparams("arbitrary"),
        name="ffn_in",
    )(x1, mod3, g.reshape(1, D), w_router_pad)


def _route_kernel(lg_ref, b_ref, tri_ref, e_ref, r_ref, w_ref, cnt_ref, run_scr):
    tm = lg_ref.shape[0]
    E, GS, NG = N_EXPERTS, GROUP_SIZE, N_GROUPS

    @pl.when(pl.program_id(0) == 0)
    def _():
        run_scr[...] = jnp.zeros_like(run_scr)

    s = jax.nn.sigmoid(jnp.transpose(lg_ref[...])[:E, :])
    sb = s + b_ref[:, 0:1]
    gi = lax.broadcasted_iota(I32, (NG, tm), 0)
    gs = jnp.zeros((NG, tm), F32)
    for j in range(NG):
        blk = sb[j * GS:(j + 1) * GS, :]
        m1 = jnp.max(blk, axis=0, keepdims=True)
        first = jnp.min(jnp.where(blk == m1, gi, GS), axis=0, keepdims=True)
        m2 = jnp.max(jnp.where(gi == first, -jnp.inf, blk), axis=0, keepdims=True)
        gs = jnp.where(gi == j, m1 + m2, gs)

    grank = jnp.zeros((NG, tm), I32)
    for j in range(NG):
        rowv = gs[j:j + 1, :]
        grank = grank + ((rowv > gs) | ((rowv == gs) & (gi > j))).astype(I32)
    gsel = jnp.where(grank < TOP_GROUPS, 1.0, 0.0)

    ei = lax.broadcasted_iota(I32, (E, tm), 0)
    esel = jnp.zeros((E, tm), F32)
    for j in range(NG):
        esel = jnp.where(ei // GS == j, gsel[j:j + 1, :], esel)
    masked = jnp.where(esel > 0.5, sb, NEG)

    erank = jnp.zeros((E, tm), I32)
    for j in range(E):
        rowv = masked[j:j + 1, :]
        erank = erank + ((rowv > masked) | ((rowv == masked) & (ei > j))).astype(I32)
    selm = erank < TOP_K

    wsum = jnp.sum(jnp.where(selm, s, 0.0), axis=0, keepdims=True)
    wfull = jnp.where(selm, s / wsum * ROUTE_SCALE, 0.0)

    self32 = selm.astype(F32)
    before = _dot(self32.astype(BF16), tri_ref[...]) + run_scr[:, 0:1]
    run_new = run_scr[:, 0:1] + jnp.sum(self32, axis=1, keepdims=True)
    run_scr[...] = jnp.broadcast_to(run_new, run_scr.shape)
    cnt_ref[...] = jnp.broadcast_to(run_new, cnt_ref.shape).astype(I32)

    before_i = before.astype(I32)
    for k in range(TOP_K):
        hit = selm & (erank == k)
        e_ref[k:k + 1, :] = jnp.sum(jnp.where(hit, ei, 0), axis=0, keepdims=True)
        r_ref[k:k + 1, :] = jnp.sum(jnp.where(hit, before_i, 0), axis=0, keepdims=True)
        w_ref[k:k + 1, :] = jnp.sum(jnp.where(hit, wfull, 0.0), axis=0, keepdims=True)


def _route(logits, b_router):
    N = logits.shape[0]
    tm = 256
    tri = jnp.asarray(np.triu(np.ones((tm, tm), np.float32), 1), BF16)
    b_col = jnp.broadcast_to(b_router.reshape(N_EXPERTS, 1), (N_EXPERTS, LANES))
    kspec = pl.BlockSpec((TOP_K, tm), lambda i: (0, i))
    return pl.pallas_call(
        _route_kernel,
        grid=(N // tm,),
        in_specs=[pl.BlockSpec((tm, LANES), lambda i: (i, 0)),
                  pl.BlockSpec((N_EXPERTS, LANES), lambda i: (0, 0)),
                  pl.BlockSpec((tm, tm), lambda i: (0, 0))],
        out_specs=[kspec, kspec, kspec, pl.BlockSpec((N_EXPERTS, LANES), lambda i: (0, 0))],
        out_shape=[jax.ShapeDtypeStruct((TOP_K, N), I32), jax.ShapeDtypeStruct((TOP_K, N), I32),
                   jax.ShapeDtypeStruct((TOP_K, N), F32), jax.ShapeDtypeStruct((N_EXPERTS, LANES), I32)],
        scratch_shapes=[pltpu.VMEM((N_EXPERTS, LANES), F32)],
        compiler_params=_cparams("arbitrary"),
        name="moe_route",
    )(logits, b_col, tri)


def _dispatch_kernel(dst_ref, ps_ref, cnt_ref, pe_ref, h_ref, wgu_ref, wd_ref, weg_ref, weu_ref,
                     xs_ref, sh_ref, wegb_ref, weub_ref, zero_scr, sem, zsem):
    i = pl.program_id(0)
    tm = h_ref.shape[0] // ROW_TILE

    def tile_at(ref, r):
        return ref.at[pl.ds(pl.multiple_of(r * ROW_TILE, ROW_TILE), ROW_TILE), :]

    @pl.when(i == 0)
    def _():
        _store_zero_rows(zero_scr)

        def per_expert(e, n):
            lo = ps_ref[e] + cnt_ref[e]
            hi = pe_ref[e]

            def issue(rw, c):
                pltpu.make_async_copy(tile_at(zero_scr, 0), tile_at(xs_ref, rw), zsem).start()
                return c
            lax.fori_loop(lo, hi, issue, 0)
            return n + (hi - lo)
        n_pad = lax.fori_loop(0, N_EXPERTS, per_expert, 0)

        def drain(_, c):
            pltpu.make_async_copy(tile_at(zero_scr, 0), tile_at(xs_ref, 0), zsem).wait()
            return c
        lax.fori_loop(0, n_pad, drain, 0)

        blk_rows = zero_scr.shape[0]

        def tail_copy(b):
            return pltpu.make_async_copy(
                zero_scr, xs_ref.at[pl.ds(pl.multiple_of(b * blk_rows, blk_rows), blk_rows), :], zsem)
        first_free = pe_ref[N_EXPERTS - 1] * ROW_TILE // blk_rows
        n_blocks = xs_ref.shape[0] // blk_rows
        lax.fori_loop(first_free, n_blocks, lambda b, c: (tail_copy(b).start(), c)[1], 0)
        lax.fori_loop(first_free, n_blocks, lambda b, c: (tail_copy(b).wait(), c)[1], 0)

    def issue_row(r, c):
        src = tile_at(h_ref, r)
        for k in range(TOP_K):
            pltpu.make_async_copy(src, tile_at(xs_ref, dst_ref[k, r]), sem).start()
        return c
    lax.fori_loop(0, tm, issue_row, 0)

    wegb_ref[...] = weg_ref[...].astype(BF16)
    weub_ref[...] = weu_ref[...].astype(BF16)
    words = jnp.concatenate([_load_row_chunk(h_ref, c, tm) for c in range(ROW_TILE)], axis=1)
    lo, hi = _unpack_rows(words)
    lo, hi = lo.astype(BF16), hi.astype(BF16)
    half = lo.shape[1]
    gu = _dot(lo, wgu_ref[:half, :]) + _dot(hi, wgu_ref[half:, :])
    ff = gu.shape[1] // 2
    gate, up = gu[:, :ff], gu[:, ff:]
    act = (gate * jax.nn.sigmoid(gate) * up).astype(BF16)
    sh_ref[...] = _dot(act, wd_ref[...]).astype(BF16)

    for k in range(TOP_K):
        pltpu.make_async_copy(h_ref, xs_ref.at[pl.ds(0, tm * ROW_TILE), :], sem).wait()


def _dispatch(dest, pad_start, counts, pad_end, h_packed, w_sh_gu, w_sh_down, w_e_gate, w_e_up, n_rows):
    N = h_packed.shape[0] // ROW_TILE
    D = w_sh_gu.shape[0]
    E, _, FF = w_e_gate.shape
    tm = N // E
    assert tm % LANES == 0
    wspec = lambda: pl.BlockSpec((1, D, FF), lambda i: (i, 0, 0))
    whole = lambda: pl.BlockSpec(memory_space=pltpu.SMEM)
    return pl.pallas_call(
        _dispatch_kernel,
        grid=(N // tm,),
        in_specs=[pl.BlockSpec((TOP_K, tm), lambda i: (0, i), memory_space=pltpu.SMEM),
                  whole(), whole(), whole(),
                  pl.BlockSpec((tm * ROW_TILE, LANES), lambda i: (i, 0)),
                  pl.BlockSpec(w_sh_gu.shape, lambda i: (0, 0)),
                  pl.BlockSpec(w_sh_down.shape, lambda i: (0, 0)),
                  wspec(), wspec()],
        out_specs=[pl.BlockSpec(memory_space=pl.ANY), pl.BlockSpec((tm, D), lambda i: (i, 0)),
                   wspec(), wspec()],
        out_shape=[jax.ShapeDtypeStruct((n_rows * ROW_TILE, LANES), U32),
                   jax.ShapeDtypeStruct((N, D), BF16),
                   jax.ShapeDtypeStruct(w_e_gate.shape, BF16),
                   jax.ShapeDtypeStruct(w_e_up.shape, BF16)],
        scratch_shapes=[pltpu.VMEM((MOE_BLK * ROW_TILE, LANES), U32),
                        pltpu.SemaphoreType.DMA, pltpu.SemaphoreType.DMA],
        compiler_params=_cparams("arbitrary"),
        name="moe_dispatch",
    )(dest, pad_start, counts, pad_end, h_packed, w_sh_gu, w_sh_down, w_e_gate, w_e_up)


def _expert_kernel(bexp_ref, nused_ref, xs_ref, wg_ref, wu_ref, wd_ref, ys_ref, wd_scr):
    i = pl.program_id(0)
    used = i < nused_ref[0]

    @pl.when(jnp.logical_not(used))
    def _():
        _store_zero_rows(ys_ref)

    @pl.when(used & ((i == 0) | (bexp_ref[i] != bexp_ref[jnp.maximum(i - 1, 0)])))
    def _():
        wd_scr[...] = wd_ref[0].astype(BF16)

    @pl.when(used)
    def _():
        m = xs_ref.shape[0] // ROW_TILE
        words = jnp.concatenate([_load_row_chunk(xs_ref, c, m) for c in range(ROW_TILE)], axis=1)
        lo, hi = _unpack_rows(words)
        lo, hi = lo.astype(BF16), hi.astype(BF16)
        half = lo.shape[1]
        gate = _dot(lo, wg_ref[0, :half, :]) + _dot(hi, wg_ref[0, half:, :])
        up = _dot(lo, wu_ref[0, :half, :]) + _dot(hi, wu_ref[0, half:, :])
        act = (gate * jax.nn.sigmoid(gate) * up).astype(BF16)
        _store_rows(ys_ref, _pack_rows(_dot(act, wd_scr[...])))


def _experts(blk_exp, n_used, xs, w_gate, w_up, w_down):
    P = xs.shape[0] // ROW_TILE
    n_blk = P // MOE_BLK
    D, FF = w_gate.shape[1], w_gate.shape[2]
    row_map = lambda i, be, nu: (jnp.minimum(i, nu[0] - 1), 0)
    return pl.pallas_call(
        _expert_kernel,
        grid_spec=pltpu.PrefetchScalarGridSpec(
            num_scalar_prefetch=2,
            grid=(n_blk,),
            in_specs=[pl.BlockSpec((MOE_BLK * ROW_TILE, LANES), row_map),
                      pl.BlockSpec((1, D, FF), lambda i, be, nu: (be[i], 0, 0)),
                      pl.BlockSpec((1, D, FF), lambda i, be, nu: (be[i], 0, 0)),
                      pl.BlockSpec((1, FF, D), lambda i, be, nu: (be[i], 0, 0))],
            out_specs=pl.BlockSpec((MOE_BLK * ROW_TILE, LANES), lambda i, be, nu: (i, 0)),
            scratch_shapes=[pltpu.VMEM((FF, D), BF16)]),
        out_shape=jax.ShapeDtypeStruct(xs.shape, U32),
        compiler_params=_cparams("arbitrary"),
        name="moe_experts",
    )(blk_exp, n_used, xs, w_gate, w_up, w_down)


def _combine_kernel(dcur_ref, dnxt_ref, w_ref, sh_ref, x_ref, mod_ref, g_ref, ys_ref, o_ref, buf2, sems):
    i = pl.program_id(0)
    n = pl.num_programs(0)
    tm = x_ref.shape[0]
    half = ROW_TILE * LANES

    def tile_at(ref, r):
        return ref.at[pl.ds(pl.multiple_of(r * ROW_TILE, ROW_TILE), ROW_TILE), :]

    def gather(slots_ref, slot):
        def issue_row(r, c):
            for k in range(TOP_K):
                pltpu.make_async_copy(tile_at(ys_ref, slots_ref[k, r]),
                                      tile_at(buf2.at[slot, k], r), sems.at[slot]).start()
            return c
        lax.fori_loop(0, tm, issue_row, 0)

    @pl.when(i == 0)
    def _():
        gather(dcur_ref, 0)

    @pl.when(i + 1 < n)
    def _():
        gather(dnxt_ref, (i + 1) % 2)

    slot = i % 2
    buf = buf2.at[slot]
    for k in range(TOP_K):
        pltpu.make_async_copy(ys_ref.at[pl.ds(0, tm * ROW_TILE), :], buf.at[k], sems.at[slot]).wait()

    w = w_ref[...]
    ssq = jnp.zeros((tm, 1), F32)
    for c in range(ROW_TILE):
        c_lo = slice(c * LANES, (c + 1) * LANES)
        c_hi = slice(half + c * LANES, half + (c + 1) * LANES)
        acc_lo = sh_ref[:, c_lo].astype(F32)
        acc_hi = sh_ref[:, c_hi].astype(F32)
        for k in range(TOP_K):
            lo, hi = _unpack_rows(_load_row_chunk(buf.at[k], c, tm))
            wk = w[:, k:k + 1]
            acc_lo = acc_lo + wk * lo
            acc_hi = acc_hi + wk * hi
        o_ref[:, c_lo] = acc_lo
        o_ref[:, c_hi] = acc_hi
        ssq = ssq + jnp.sum(acc_lo * acc_lo, axis=-1, keepdims=True) + jnp.sum(acc_hi * acc_hi, axis=-1, keepdims=True)
    inv = lax.rsqrt(ssq / (2 * half) + EPS)
    o_ref[...] = x_ref[...] + mod_ref[0, 5:6, :] * (o_ref[...] * inv * g_ref[...])


def _combine(dest, w_tok, shared, x1, mod3, g, ys, T):
    N, D = x1.shape
    tm = 256
    per_b = T // tm
    last = N // tm - 1
    return pl.pallas_call(
        _combine_kernel,
        grid=(N // tm,),
        in_specs=[pl.BlockSpec((TOP_K, tm), lambda i: (0, i), memory_space=pltpu.SMEM),
                  pl.BlockSpec((TOP_K, tm), lambda i: (0, jnp.minimum(i + 1, last)), memory_space=pltpu.SMEM),
                  pl.BlockSpec((tm, TOP_K), lambda i: (i, 0)),
                  pl.BlockSpec((tm, D), lambda i: (i, 0)),
                  pl.BlockSpec((tm, D), lambda i: (i, 0)),
                  pl.BlockSpec((1, 6, D), lambda i: (i // per_b, 0, 0)),
                  pl.BlockSpec((1, D), lambda i: (0, 0)),
                  pl.BlockSpec(memory_space=pl.ANY)],
        out_specs=pl.BlockSpec((tm, D), lambda i: (i, 0)),
        out_shape=jax.ShapeDtypeStruct((N, D), F32),
        scratch_shapes=[pltpu.VMEM((2, TOP_K, tm * ROW_TILE, LANES), U32), pltpu.SemaphoreType.DMA((2,))],
        compiler_params=_cparams("arbitrary"),
        name="moe_combine",
    )(dest, dest, w_tok, shared, x1, mod3, g.reshape(1, D), ys)


def _inproj_columns():
    starts = np.concatenate([[0], np.cumsum(IN_SIZES)])
    s_q, s_kv, s_ga, s_qkm, s_vm, s_if, s_om, s_a, s_b = starts[:9]
    cols = np.full((PROJ_W,), D_IN, np.int64)
    cols[OFF_Q:OFF_Q + NSA_Q] = s_q + np.arange(NSA_Q)
    cols[OFF_GATE_A:OFF_GATE_A + D_MODEL] = s_a + np.arange(D_MODEL)
    cols[OFF_GATE_B:OFF_GATE_B + D_MODEL] = s_b + np.arange(D_MODEL)
    for br in range(3):
        for g in range(NSA_KV_GROUPS):
            for kv in range(2):
                dst = OFF_KV + (br * NSA_KV_GROUPS + g) * LANES + kv * NSA_DH
                src = s_kv + (2 * br + kv) * NSA_KV + g * NSA_DH
                cols[dst:dst + NSA_DH] = src + np.arange(NSA_DH)
    cols[OFF_OM:OFF_OM + M_V] = s_om + np.arange(M_V)
    cols[OFF_VM:OFF_VM + M_V] = s_vm + np.arange(M_V)
    cols[OFF_QKM:OFF_QKM + 2 * M_QK] = s_qkm + np.arange(2 * M_QK)
    per_g = 3 * NSA_HPG
    for g in range(NSA_KV_GROUPS):
        cols[OFF_GA + g * LANES:OFF_GA + g * LANES + per_g] = s_ga + g * per_g + np.arange(per_g)
    cols_if = np.full((LANES,), D_IN, np.int64)
    cols_if[:2 * M_HEADS] = s_if + np.arange(2 * M_HEADS)
    return cols, cols_if


def _take_columns(w, cols):
    pieces = []
    start = 0
    for i in range(1, len(cols) + 1):
        run_ends = (i == len(cols) or (cols[start] == D_IN) != (cols[i] == D_IN)
                    or (cols[start] != D_IN and cols[i] != cols[i - 1] + 1))
        if run_ends:
            if cols[start] == D_IN:
                pieces.append(jnp.zeros((w.shape[0], i - start), w.dtype))
            else:
                pieces.append(w[:, int(cols[start]):int(cols[start]) + i - start])
            start = i
    return jnp.concatenate(pieces, axis=1)


def _cmp_to_sel(n_cmp, n_sel):
    ci = np.arange(n_cmp)[:, None]
    sj = np.arange(n_sel)[None, :]
    overlap = (np.minimum(ci * CMP_STRIDE + CMP_LEN, (sj + 1) * SEL_LEN)
               - np.maximum(ci * CMP_STRIDE, sj * SEL_LEN))
    m = np.zeros((LANES, LANES), np.float32)
    m[:n_cmp, :n_sel] = np.clip(overlap, 0, None).astype(np.float32) / CMP_LEN
    return m


def _layer(x, mod3, g_pre_mix, g_post_mix, w_in, cmp_pe, cmp_w1, cmp_w2, conv_w, conv_b, b_gates_m,
           mh_norm_g, w_up_nsa, w_up_mlstm, w_out, g_pre_ffn, g_post_ffn, w_router, b_router,
           w_e_gate, w_e_up, w_e_down, w_sh_gate, w_sh_up, w_sh_down):
    B, T, D = x.shape
    N = B * T
    x2 = x.reshape(N, D)

    cols, cols_if = _inproj_columns()
    w_in_b = w_in.astype(BF16)
    proj, gates = _inproj(x2, mod3, g_pre_mix, _take_columns(w_in_b, cols), _take_columns(w_in_b, cols_if), T)
    proj3 = proj.reshape(B, T, PROJ_W)

    G = NSA_KV_GROUPS
    nb = T // CMP_STRIDE
    kvc_in = proj3[:, :, OFF_KV:OFF_KV + G * LANES].reshape(B, nb, CMP_STRIDE, G, 2, NSA_DH)
    kvc_in = kvc_in.transpose(4, 0, 3, 1, 2, 5).reshape(2, B, G, nb, CMP_STRIDE * NSA_DH)
    pe2 = cmp_pe.reshape(2, 2, CMP_STRIDE * NSA_DH)
    w2x = jnp.stack([jnp.pad(cmp_w2[0], ((0, 0), (0, NSA_DH))),
                     jnp.pad(cmp_w2[1], ((0, 0), (NSA_DH, 0)))]).astype(BF16)
    kvc = _compress(kvc_in[0], kvc_in[1], pe2, cmp_w1.astype(BF16), w2x)

    n_sel = T // SEL_LEN
    o_nsa = _nsa_attention(proj3, kvc, jnp.asarray(_cmp_to_sel(nb - 1, n_sel).T))

    L = min(MLSTM_CHUNK, T)
    gates3 = gates.reshape(B, T, LANES)
    gates_row = gates3[:, :, :2 * M_HEADS].reshape(B, T // L, L, 2 * M_HEADS).transpose(0, 1, 3, 2)
    o_ml = _mlstm(proj3, gates3, gates_row, conv_w, conv_b, b_gates_m, mh_norm_g)

    merged = _merge(o_nsa.reshape(N, NSA_Q), o_ml.reshape(N, M_V), proj,
                    w_up_nsa.astype(BF16), w_up_mlstm.astype(BF16))
    x1 = _outproj(merged, w_out.astype(BF16), x2, mod3, g_post_mix, T)

    w_router_pad = jnp.pad(w_router, ((0, 0), (0, LANES - N_EXPERTS)))
    w_sh_gu = jnp.concatenate([w_sh_gate, w_sh_up], axis=1).astype(BF16)
    h_packed, logits = _ffn_in(x1, mod3, g_pre_ffn, w_router_pad, T)
    eidx, rnk, w_k, counts2 = _route(logits, b_router)

    counts = counts2[:, 0]
    padded = (counts + MOE_BLK - 1) // MOE_BLK * MOE_BLK
    pad_end = jnp.cumsum(padded).astype(I32)
    pad_start = pad_end - padded
    n_blk = (N * TOP_K) // MOE_BLK + N_EXPERTS
    blk_first = jnp.arange(n_blk, dtype=I32) * MOE_BLK
    blk_exp = jnp.minimum(jnp.sum((pad_end[None, :] <= blk_first[:, None]).astype(I32), axis=1), N_EXPERTS - 1)
    n_used = (pad_end[-1:] // MOE_BLK).astype(I32)
    experts = jnp.arange(N_EXPERTS, dtype=I32)
    dest = rnk + jnp.sum(jnp.where(eidx[:, :, None] == experts, pad_start, 0), axis=-1)

    xs, shared, w_gate_b, w_up_b = _dispatch(dest, pad_start, counts, pad_end, h_packed, w_sh_gu,
                                             w_sh_down.astype(BF16), w_e_gate, w_e_up, n_blk * MOE_BLK)
    ys = _experts(blk_exp, n_used, xs, w_gate_b, w_up_b, w_e_down)
    out = _combine(dest, jnp.transpose(w_k), shared, x1, mod3, g_post_ffn, ys, T)
    return out.reshape(B, T, D)


def kernel(x, c, w_ada, b_ada, g_pre_mix, g_post_mix, w_in, cmp_pe, cmp_w1, cmp_w2, conv_w, conv_b, b_gates_m, mh_norm_g, w_up_nsa, w_up_mlstm, w_out, g_pre_ffn, g_post_ffn, w_router, b_router, w_e_gate, w_e_up, w_e_down, w_sh_gate, w_sh_up, w_sh_down):
    B = x.shape[0]
    for l in range(w_ada.shape[0]):
        mod3 = _ada(c, w_ada[l], b_ada[l]).reshape(B, 6, D_MODEL)
        x = _layer(x, mod3, g_pre_mix[l], g_post_mix[l], w_in[l], cmp_pe[l], cmp_w1[l], cmp_w2[l],
                   conv_w[l], conv_b[l], b_gates_m[l], mh_norm_g[l], w_up_nsa[l], w_up_mlstm[l], w_out[l],
                   g_pre_ffn[l], g_post_ffn[l], w_router[l], b_router[l], w_e_gate[l], w_e_up[l],
                   w_e_down[l], w_sh_gate[l], w_sh_up[l], w_sh_down[l])
    return x
```

```python
import functools

import numpy as np
import jax
import jax.numpy as jnp
from jax import lax
from jax.experimental import pallas as pl
from jax.experimental.pallas import tpu as pltpu

F32 = jnp.float32
BF16 = jnp.bfloat16
I32 = jnp.int32
U32 = jnp.uint32

D_MODEL = 2048
NSA_HEADS = 16
NSA_KV_GROUPS = 4
NSA_HPG = NSA_HEADS // NSA_KV_GROUPS
NSA_DH = 64
NSA_Q = NSA_HEADS * NSA_DH
NSA_KV = NSA_KV_GROUPS * NSA_DH
CMP_STRIDE = 16
CMP_LEN = 2 * CMP_STRIDE
CMP_HID = 256
SEL_LEN = 64
SEL_TOP = 16
WIN = 512
FORCE = 1e3
NEG = -1e9
M_HEADS = 4
M_DQK = 128
M_DV = 256
M_QK = M_HEADS * M_DQK
M_V = M_HEADS * M_DV
CONV_W = 4
N_EXPERTS = 64
TOP_K = 8
N_GROUPS = 8
GROUP_SIZE = N_EXPERTS // N_GROUPS
TOP_GROUPS = 4
EXPERT_FF = 512
SHARED_FF = 512
ROUTE_SCALE = 2.5
EPS = 1e-6

IN_SIZES = (NSA_Q, 6 * NSA_KV, 3 * NSA_HEADS, 2 * M_QK, M_V, 2 * M_HEADS, M_V, D_MODEL, D_MODEL)
D_IN = sum(IN_SIZES)

LANES = 128
VMEM_LIMIT = 56 * 1024 * 1024

OFF_Q = 0
OFF_KV = 1024
OFF_OM = 2560
OFF_GA = 3584
OFF_GATE_A = 4096
OFF_GATE_B = 6144
OFF_VM = 8192
OFF_QKM = 9216
PROJ_W = 10240

MLSTM_CHUNK = 256
ATT_TILE = 256
MOE_BLK = 256
ROW_TILE = 8


def _cparams(*sem):
    return pltpu.CompilerParams(dimension_semantics=sem, vmem_limit_bytes=VMEM_LIMIT)


def _dot(a, b, **kw):
    return jnp.dot(a, b, preferred_element_type=F32, **kw)


def _dot_nt(a, b):
    return lax.dot_general(a, b, (((1,), (1,)), ((), ())), preferred_element_type=F32)


def _ada_kernel(c_ref, w_ref, b_ref, o_ref):
    c = c_ref[...]
    sc = (c * jax.nn.sigmoid(c)).astype(BF16)
    o_ref[...] = _dot(sc, w_ref[...].astype(BF16)) + b_ref[...]


def _ada(c, w_ada, b_ada):
    B, D = c.shape
    n_out = w_ada.shape[1]
    tn = 1024
    return pl.pallas_call(
        _ada_kernel,
        grid=(n_out // tn,),
        in_specs=[pl.BlockSpec((B, D), lambda j: (0, 0)),
                  pl.BlockSpec((D, tn), lambda j: (0, j)),
                  pl.BlockSpec((1, tn), lambda j: (0, j))],
        out_specs=pl.BlockSpec((B, tn), lambda j: (0, j)),
        out_shape=jax.ShapeDtypeStruct((B, n_out), F32),
        compiler_params=_cparams("arbitrary"),
        name="ada_mod",
    )(c, w_ada, b_ada.reshape(1, n_out))


def _norm_mod(x, g, shift, scale):
    ms = jnp.mean(x * x, axis=-1, keepdims=True)
    y = x * lax.rsqrt(ms + EPS) * g
    return y * (1.0 + scale) + shift


def _inproj_kernel(x_ref, mod_ref, g_ref, w_ref, wif_ref, o_ref, oif_ref, h_scr):
    @pl.when(pl.program_id(1) == 0)
    def _():
        h = _norm_mod(x_ref[...], g_ref[...], mod_ref[0, 0:1, :], mod_ref[0, 1:2, :])
        hb = h.astype(BF16)
        h_scr[...] = hb
        oif_ref[...] = _dot(hb, wif_ref[...])

    o_ref[...] = _dot(h_scr[...], w_ref[...]).astype(BF16)


def _inproj(x2, mod3, g, w_main, w_if, T):
    N, D = x2.shape
    tm, tn = min(1024, T), 1024
    per_b = T // tm
    return pl.pallas_call(
        _inproj_kernel,
        grid=(N // tm, PROJ_W // tn),
        in_specs=[pl.BlockSpec((tm, D), lambda i, j: (i, 0)),
                  pl.BlockSpec((1, 6, D), lambda i, j: (i // per_b, 0, 0)),
                  pl.BlockSpec((1, D), lambda i, j: (0, 0)),
                  pl.BlockSpec((D, tn), lambda i, j: (0, j)),
                  pl.BlockSpec((D, LANES), lambda i, j: (0, 0))],
        out_specs=[pl.BlockSpec((tm, tn), lambda i, j: (i, j)),
                   pl.BlockSpec((tm, LANES), lambda i, j: (i, 0))],
        out_shape=[jax.ShapeDtypeStruct((N, PROJ_W), BF16),
                   jax.ShapeDtypeStruct((N, LANES), F32)],
        scratch_shapes=[pltpu.VMEM((tm, D), BF16)],
        compiler_params=_cparams("arbitrary", "arbitrary"),
        name="in_proj",
    )(x2, mod3, g.reshape(1, D), w_main, w_if)


def _compress_kernel(kv_ref, pe_ref, w1_ref, w2_ref, o_ref, x_scr):
    nb = kv_ref.shape[1] // CMP_STRIDE
    x_scr[...] = kv_ref[0].astype(F32)
    a = jnp.zeros((nb, 2 * CMP_HID), F32)
    b = jnp.zeros((nb, 2 * CMP_HID), F32)
    for pos in range(CMP_STRIDE):
        x = x_scr[pl.ds(pos, nb, stride=CMP_STRIDE), :]
        a = a + _dot((x + pe_ref[pos:pos + 1, :]).astype(BF16), w1_ref[pos])
        b = b + _dot((x + pe_ref[CMP_STRIDE + pos:CMP_STRIDE + pos + 1, :]).astype(BF16),
                     w1_ref[CMP_STRIDE + pos])
    hid = jax.nn.gelu(a + pltpu.roll(b, nb - 1, axis=0))
    o_ref[0, 0] = _dot(hid.astype(BF16), w2_ref[...]).astype(BF16)


def _compress(proj3, cmp_pe, cmp_w1, cmp_w2):
    B, T, _ = proj3.shape
    nb = T // CMP_STRIDE
    pe_cat = jnp.concatenate([cmp_pe[0], cmp_pe[1]], axis=-1)
    w1r = cmp_w1.astype(BF16).reshape(2, CMP_LEN, NSA_DH, CMP_HID)
    zero = jnp.zeros_like(w1r[0])
    w1x = jnp.concatenate([jnp.concatenate([w1r[0], zero], axis=-1),
                           jnp.concatenate([zero, w1r[1]], axis=-1)], axis=1)
    w2x = jnp.concatenate([jnp.pad(cmp_w2[0], ((0, 0), (0, NSA_DH))),
                           jnp.pad(cmp_w2[1], ((0, 0), (NSA_DH, 0)))], axis=0).astype(BF16)
    return pl.pallas_call(
        _compress_kernel,
        grid=(B, NSA_KV_GROUPS),
        in_specs=[pl.BlockSpec((1, T, LANES), lambda b, g: (b, 0, OFF_KV // LANES + g)),
                  pl.BlockSpec(pe_cat.shape, lambda b, g: (0, 0)),
                  pl.BlockSpec(w1x.shape, lambda b, g: (0, 0, 0)),
                  pl.BlockSpec(w2x.shape, lambda b, g: (0, 0))],
        out_specs=pl.BlockSpec((1, 1, nb, LANES), lambda b, g: (b, g, 0, 0)),
        out_shape=jax.ShapeDtypeStruct((B, NSA_KV_GROUPS, nb, LANES), BF16),
        scratch_shapes=[pltpu.VMEM((T, LANES), F32)],
        compiler_params=_cparams("arbitrary", "arbitrary"),
        name="nsa_compress",
    )(proj3, pe_cat, w1x, w2x)


def _nsa_kernel(q_ref, kvc_ref, kvs_ref, kvw_ref, ga_ref, c2st_ref, o_ref,
                kx_scr, tri_scr, m_scr, l_scr, acc_scr, *, n_cmp, n_sel, top):
    qi = pl.program_id(2)
    tq = q_ref.shape[1]
    tk = tq
    T = kvs_ref.shape[1]
    H = NSA_HPG
    scale = NSA_DH ** -0.5
    n_sel_pad = -(-n_sel // 8) * 8

    @pl.when(qi == 0)
    def _():
        rt = lax.broadcasted_iota(I32, (T, LANES), 0)
        lt = lax.broadcasted_iota(I32, (T, LANES), 1)
        onehot = jnp.where(lt - NSA_DH == rt // SEL_LEN, 1.0, 0.0)
        kx_scr[...] = jnp.where(lt < NSA_DH, kvs_ref[0].astype(F32), onehot).astype(BF16)
        r2 = lax.broadcasted_iota(I32, (H * tq, tk), 0) % tq
        c2 = lax.broadcasted_iota(I32, (H * tq, tk), 1)
        tri_scr[0] = jnp.where(c2 <= r2, 0.0, NEG)
        tri_scr[1] = jnp.where(c2 > r2, 0.0, NEG)

    row = lax.broadcasted_iota(I32, (tq, LANES), 0)
    lane = lax.broadcasted_iota(I32, (tq, LANES), 1)
    t_abs = qi * tq + row
    lo = lane < NSA_DH

    q2 = q_ref[0].astype(F32) * scale
    qh = []
    for pr in range(H // 2):
        qp = q2[:, pr * LANES:(pr + 1) * LANES]
        qh.append(qp)
        qh.append(pltpu.roll(qp, NSA_DH, axis=1))
    qz = jnp.concatenate([jnp.where(lo, x, 0.0) for x in qh], axis=0).astype(BF16)

    kvc = kvc_ref[0, 0]
    cmp_ok = (lane < n_cmp) & (lane * CMP_STRIDE + (CMP_LEN - 1) <= t_abs)
    cmp_ok = jnp.concatenate([cmp_ok] * H, axis=0)
    s = jnp.where(cmp_ok, _dot_nt(qz, kvc), NEG)
    m = jnp.max(s, axis=-1, keepdims=True)
    p = jnp.where(cmp_ok, jnp.exp(s - m), 0.0)
    l = jnp.sum(p, axis=-1, keepdims=True)
    p = p / jnp.where(l > 0.0, l, 1.0)
    acc_scr[2] = _dot(p.astype(BF16), kvc)
    psum = p[0:tq]
    for h in range(1, H):
        psum = psum + p[h * tq:(h + 1) * tq]

    imp_t = lax.dot_general(c2st_ref[...], psum, (((1,), (1,)), ((), ())),
                            preferred_element_type=F32, precision=lax.Precision.HIGHEST)[:n_sel_pad, :]
    blk = lax.broadcasted_iota(I32, (n_sel_pad, tq), 0)
    cur = (qi * tq + lax.broadcasted_iota(I32, (n_sel_pad, tq), 1)) // SEL_LEN
    valid = (blk <= cur) & (blk < n_sel)
    forced = (blk == 0) | (blk == cur) | (blk == cur - 1)
    score = jnp.where(valid, imp_t + jnp.where(forced, FORCE, 0.0), NEG)
    rank = jnp.zeros((n_sel_pad, tq), I32)
    for j in range(n_sel):
        rowv = score[j:j + 1, :]
        rank = rank + ((rowv > score) | ((rowv == score) & (blk > j))).astype(I32)
    nsel_t = jnp.where(valid & (rank < top), 0.0, NEG)
    nsel_t = jnp.concatenate([nsel_t, jnp.zeros((LANES - n_sel_pad, tq), F32)], axis=0)
    nsel = pltpu.roll(jnp.transpose(nsel_t), NSA_DH, axis=1)
    qs = jnp.concatenate([jnp.where(lo, x, nsel) for x in qh], axis=0).astype(BF16)

    m_scr[...] = jnp.full(m_scr.shape, -1e30, F32)
    l_scr[...] = jnp.zeros(l_scr.shape, F32)
    acc_scr[0:2] = jnp.zeros((2, H * tq, LANES), F32)

    def step(br, queries, k_tile, v_tile, bias):
        s = _dot_nt(queries, k_tile)
        if bias is not None:
            s = s + bias
        m_prev = m_scr[br]
        m_new = jnp.maximum(m_prev, jnp.max(s, axis=-1, keepdims=True))
        alpha = jnp.exp(m_prev - m_new)
        p = jnp.exp(s - jnp.concatenate([m_new] * (tk // LANES), axis=1))
        l_scr[br] = alpha * l_scr[br] + jnp.sum(p, axis=-1, keepdims=True)
        acc_scr[br] = alpha * acc_scr[br] + _dot(p.astype(BF16), v_tile)
        m_scr[br] = m_new

    def sel_tile(kt, carry):
        off = pl.multiple_of(kt * tk, tk)
        step(0, qs, kx_scr[pl.ds(off, tk), :], kvs_ref[0, pl.ds(off, tk), :], None)
        return carry
    lax.fori_loop(0, qi, sel_tile, 0)

    def win_tile(back, bias):
        off = pl.multiple_of((qi - back) * tk, tk)
        w = kvw_ref[0, pl.ds(off, tk), :]
        step(1, qz, w, w, bias)

    @pl.when(qi >= 2)
    def _():
        win_tile(2, tri_scr[1])

    @pl.when(qi >= 1)
    def _():
        win_tile(1, None)

    diag = pl.multiple_of(qi * tk, tk)
    step(0, qs, kx_scr[pl.ds(diag, tk), :], kvs_ref[0, pl.ds(diag, tk), :], tri_scr[0])
    win_tile(0, tri_scr[0])

    gates = jax.nn.sigmoid(ga_ref[0].astype(F32))

    def gate_rows(br):
        return jnp.concatenate([jnp.broadcast_to(gates[:, 3 * h + br:3 * h + br + 1], (tq, LANES))
                                for h in range(H)], axis=0)
    o_all = (gate_rows(0) * acc_scr[2] + gate_rows(1) * (acc_scr[0] / l_scr[0])
             + gate_rows(2) * (acc_scr[1] / l_scr[1]))
    for pr in range(H // 2):
        even = o_all[2 * pr * tq:(2 * pr + 1) * tq]
        odd = o_all[(2 * pr + 1) * tq:(2 * pr + 2) * tq]
        pair = jnp.where(lo, pltpu.roll(even, NSA_DH, axis=1), odd)
        o_ref[0, :, pr * LANES:(pr + 1) * LANES] = pair.astype(BF16)


def _nsa_attention(proj3, kvc, c2s_t):
    B, T, _ = proj3.shape
    G = NSA_KV_GROUPS
    tq = min(ATT_TILE, T)
    nq = T // tq
    n_cmp = T // CMP_STRIDE - 1
    n_sel = T // SEL_LEN
    top = min(SEL_TOP, n_sel)
    gw = NSA_HPG * NSA_DH
    assert kvc.shape[2] <= LANES and n_sel <= NSA_DH and WIN == 2 * tq
    kvc = jnp.pad(kvc, ((0, 0), (0, 0), (0, LANES - kvc.shape[2]), (0, 0)))
    kern = functools.partial(_nsa_kernel, n_cmp=n_cmp, n_sel=n_sel, top=top)
    return pl.pallas_call(
        kern,
        grid=(B, G, nq),
        in_specs=[
            pl.BlockSpec((1, tq, gw), lambda b, g, i: (b, i, OFF_Q // gw + g)),
            pl.BlockSpec((1, 1, LANES, LANES), lambda b, g, i: (b, g, 0, 0)),
            pl.BlockSpec((1, T, LANES), lambda b, g, i: (b, 0, OFF_KV // LANES + G + g)),
            pl.BlockSpec((1, T, LANES), lambda b, g, i: (b, 0, OFF_KV // LANES + 2 * G + g)),
            pl.BlockSpec((1, tq, LANES), lambda b, g, i: (b, i, OFF_GA // LANES + g)),
            pl.BlockSpec((LANES, LANES), lambda b, g, i: (0, 0)),
        ],
        out_specs=pl.BlockSpec((1, tq, gw), lambda b, g, i: (b, i, g)),
        out_shape=jax.ShapeDtypeStruct((B, T, NSA_Q), BF16),
        scratch_shapes=[pltpu.VMEM((T, LANES), BF16),
                        pltpu.VMEM((2, NSA_HPG * tq, tq), F32),
                        pltpu.VMEM((2, NSA_HPG * tq, LANES), F32),
                        pltpu.VMEM((2, NSA_HPG * tq, LANES), F32),
                        pltpu.VMEM((3, NSA_HPG * tq, LANES), F32)],
        compiler_params=_cparams("arbitrary", "arbitrary", "arbitrary"),
        name="nsa_attention",
    )(proj3, kvc, proj3, proj3, proj3, c2s_t)


def _log_sigmoid(x):
    return -(jnp.maximum(-x, 0.0) + jnp.log1p(jnp.exp(-jnp.abs(x))))


def _mlstm_kernel(q_ref, k_ref, v_ref, o_ref, gc_ref, gr_ref, cwq_ref, cwk_ref, cbq_ref, cbk_ref,
                  bg_ref, ng_ref, out_ref, q_scr, k_scr, c_scr):
    hd = pl.program_id(1)
    T = q_ref.shape[1]
    L = gr_ref.shape[3]
    nc = T // L
    dv = v_ref.shape[2]

    trow = lax.broadcasted_iota(I32, (T, M_DQK), 0)

    def conv_silu(x_ref, w_ref, b_ref):
        x = x_ref[0].astype(F32)
        y = b_ref[...] + w_ref[CONV_W - 1:CONV_W, :] * x
        for d in range(1, CONV_W):
            xs = jnp.where(trow >= d, pltpu.roll(x, d, axis=0), 0.0)
            y = y + w_ref[CONV_W - 1 - d:CONV_W - d, :] * xs
        return y * jax.nn.sigmoid(y)

    q_scr[...] = conv_silu(q_ref, cwq_ref, cbq_ref).astype(BF16)
    k_scr[...] = (conv_silu(k_ref, cwk_ref, cbk_ref) * (M_DQK ** -0.5)).astype(BF16)
    c_scr[...] = jnp.zeros_like(c_scr)

    lane_l = lax.broadcasted_iota(I32, (L, LANES), 1)
    sub8 = lax.broadcasted_iota(I32, (2 * M_HEADS, L), 0)
    r_i = lax.broadcasted_iota(I32, (L, L), 0)
    c_i = lax.broadcasted_iota(I32, (L, L), 1)
    tri = c_i <= r_i
    ones_blk = jnp.where(lane_l == 0, 1.0, 0.0).astype(BF16)

    bl = lax.broadcasted_iota(I32, (1, LANES), 1)
    bg = bg_ref[...]
    b_i = jnp.sum(jnp.where(bl == hd, bg, 0.0), axis=-1, keepdims=True)
    b_f = jnp.sum(jnp.where(bl == hd + M_HEADS, bg, 0.0), axis=-1, keepdims=True)

    def chunk(c, m):
        off = pl.multiple_of(c * L, L)
        gc = gc_ref[0, pl.ds(off, L), :]
        gr = gr_ref[0, c]
        li_col = jnp.sum(jnp.where(lane_l == hd, gc, 0.0), axis=-1, keepdims=True) + b_i
        lf_col = _log_sigmoid(jnp.sum(jnp.where(lane_l == hd + M_HEADS, gc, 0.0), axis=-1, keepdims=True) + b_f)
        li_row = jnp.sum(jnp.where(sub8 == hd, gr, 0.0), axis=0, keepdims=True) + b_i
        lf_row = _log_sigmoid(jnp.sum(jnp.where(sub8 == hd + M_HEADS, gr, 0.0), axis=0, keepdims=True) + b_f)
        a_col = jnp.sum(jnp.where(tri, lf_row, 0.0), axis=-1, keepdims=True)
        a_row = jnp.sum(jnp.where(r_i <= c_i, lf_col, 0.0), axis=0, keepdims=True)
        a_end = jnp.sum(lf_row, axis=-1, keepdims=True)

        dlog = jnp.where(tri, a_col - a_row + li_row, -jnp.inf)
        inter = a_col + m
        mt = jnp.maximum(inter, jnp.max(dlog, axis=-1, keepdims=True))
        dmat = jnp.exp(dlog - mt)
        iw = jnp.exp(inter - mt)

        qc = q_scr[pl.ds(off, L), :]
        kc = k_scr[pl.ds(off, L), :]
        v_aug = jnp.concatenate([v_ref[0, pl.ds(off, L), :], ones_blk], axis=-1)
        s = (_dot_nt(qc, kc) * dmat).astype(BF16)
        cmat = c_scr[...]
        num = iw * _dot(qc, cmat.astype(BF16)) + _dot(s, v_aug)
        den = num[:, dv:dv + 1]
        hc = num[:, :dv] / jnp.maximum(jnp.abs(den), jnp.exp(-mt))

        elog = a_end - a_col + li_col
        m_new = jnp.maximum(a_end + m, jnp.max(elog, axis=0, keepdims=True))
        ew = jnp.exp(elog - m_new)
        decay = jnp.exp(a_end + m - m_new)
        kct = jnp.transpose(kc.astype(F32)).astype(BF16)
        c_scr[...] = decay * cmat + _dot(kct, (ew * v_aug.astype(F32)).astype(BF16))

        hn = hc * lax.rsqrt(jnp.mean(hc * hc, axis=-1, keepdims=True) + EPS) * ng_ref[...]
        og = jax.nn.sigmoid(o_ref[0, pl.ds(off, L), :].astype(F32))
        out_ref[0, pl.ds(off, L), :] = (og * hn).astype(BF16)
        return m_new

    lax.fori_loop(0, nc, chunk, jnp.zeros((1, 1), F32))


def _mlstm(proj3, gates_col, gates_row, conv_w, conv_b, b_gates, norm_g):
    B, T, _ = proj3.shape
    L = gates_row.shape[3]
    nc = T // L
    H = M_HEADS
    qb = OFF_QKM // M_DQK
    return pl.pallas_call(
        _mlstm_kernel,
        grid=(B, H),
        in_specs=[
            pl.BlockSpec((1, T, M_DQK), lambda b, h: (b, 0, qb + h)),
            pl.BlockSpec((1, T, M_DQK), lambda b, h: (b, 0, qb + H + h)),
            pl.BlockSpec((1, T, M_DV), lambda b, h: (b, 0, OFF_VM // M_DV + h)),
            pl.BlockSpec((1, T, M_DV), lambda b, h: (b, 0, OFF_OM // M_DV + h)),
            pl.BlockSpec((1, T, LANES), lambda b, h: (b, 0, 0)),
            pl.BlockSpec((1, nc, 2 * H, L), lambda b, h: (b, 0, 0, 0)),
            pl.BlockSpec((CONV_W, M_DQK), lambda b, h: (0, h)),
            pl.BlockSpec((CONV_W, M_DQK), lambda b, h: (0, H + h)),
            pl.BlockSpec((1, M_DQK), lambda b, h: (0, h)),
            pl.BlockSpec((1, M_DQK), lambda b, h: (0, H + h)),
            pl.BlockSpec((1, LANES), lambda b, h: (0, 0)),
            pl.BlockSpec((1, M_DV), lambda b, h: (0, h)),
        ],
        out_specs=pl.BlockSpec((1, T, M_DV), lambda b, h: (b, 0, h)),
        out_shape=jax.ShapeDtypeStruct((B, T, M_V), BF16),
        scratch_shapes=[pltpu.VMEM((T, M_DQK), BF16), pltpu.VMEM((T, M_DQK), BF16),
                        pltpu.VMEM((M_DQK, M_DV + LANES), F32)],
        compiler_params=_cparams("arbitrary", "arbitrary"),
        name="mlstm",
    )(proj3, proj3, proj3, proj3, gates_col, gates_row, conv_w, conv_w,
      conv_b.reshape(1, -1), conv_b.reshape(1, -1),
      jnp.pad(b_gates, (0, LANES - 2 * H)).reshape(1, LANES), norm_g.reshape(1, -1))


def _merge_kernel(on_ref, om_ref, ga_ref, gb_ref, wn_ref, wm_ref, o_ref):
    a = _dot(on_ref[...], wn_ref[...])
    b = _dot(om_ref[...], wm_ref[...])
    ga = jax.nn.sigmoid(ga_ref[...].astype(F32))
    gb = jax.nn.sigmoid(gb_ref[...].astype(F32))
    o_ref[...] = (ga * a + gb * b).astype(BF16)


def _merge(o_nsa, o_ml, proj, w_up_nsa, w_up_ml):
    N = o_nsa.shape[0]
    D = D_MODEL
    tm = 512
    return pl.pallas_call(
        _merge_kernel,
        grid=(N // tm,),
        in_specs=[pl.BlockSpec((tm, NSA_Q), lambda i: (i, 0)),
                  pl.BlockSpec((tm, M_V), lambda i: (i, 0)),
                  pl.BlockSpec((tm, D), lambda i: (i, OFF_GATE_A // D)),
                  pl.BlockSpec((tm, D), lambda i: (i, OFF_GATE_B // D)),
                  pl.BlockSpec((NSA_Q, D), lambda i: (0, 0)),
                  pl.BlockSpec((M_V, D), lambda i: (0, 0))],
        out_specs=pl.BlockSpec((tm, D), lambda i: (i, 0)),
        out_shape=jax.ShapeDtypeStruct((N, D), BF16),
        compiler_params=_cparams("arbitrary"),
        name="mixer_merge",
    )(o_nsa, o_ml, proj, proj, w_up_nsa, w_up_ml)


def _outproj_kernel(m_ref, w_ref, x_ref, mod_ref, g_ref, o_ref):
    y = _dot(m_ref[...], w_ref[...])
    ms = jnp.mean(y * y, axis=-1, keepdims=True)
    o_ref[...] = x_ref[...] + mod_ref[0, 2:3, :] * (y * lax.rsqrt(ms + EPS) * g_ref[...])


def _outproj(merged, w_out, x2, mod3, g, T):
    N, D = x2.shape
    tm = 512
    per_b = T // tm
    return pl.pallas_call(
        _outproj_kernel,
        grid=(N // tm,),
        in_specs=[pl.BlockSpec((tm, D), lambda i: (i, 0)),
                  pl.BlockSpec((D, D), lambda i: (0, 0)),
                  pl.BlockSpec((tm, D), lambda i: (i, 0)),
                  pl.BlockSpec((1, 6, D), lambda i: (i // per_b, 0, 0)),
                  pl.BlockSpec((1, D), lambda i: (0, 0))],
        out_specs=pl.BlockSpec((tm, D), lambda i: (i, 0)),
        out_shape=jax.ShapeDtypeStruct((N, D), F32),
        compiler_params=_cparams("arbitrary"),
        name="out_proj",
    )(merged, w_out, x2, mod3, g.reshape(1, D))


def _pack_rows(y):
    half = y.shape[1] // 2
    return pltpu.pack_elementwise([y[:, :half], y[:, half:]], packed_dtype=BF16)


def _unpack_rows(w):
    lo = pltpu.unpack_elementwise(w, index=0, packed_dtype=BF16, unpacked_dtype=F32)
    hi = pltpu.unpack_elementwise(w, index=1, packed_dtype=BF16, unpacked_dtype=F32)
    return lo, hi


def _store_rows(ref, packed):
    m = packed.shape[0]
    for c in range(ROW_TILE):
        ref[pl.ds(c, m, stride=ROW_TILE), :] = packed[:, c * LANES:(c + 1) * LANES]


def _store_zero_rows(ref):
    zero = jnp.zeros(ref.shape, F32)
    ref[...] = pltpu.pack_elementwise([zero, zero], packed_dtype=BF16)


def _load_row_chunk(ref, c, m):
    return ref[pl.ds(c, m, stride=ROW_TILE), :]


def _ffn_in_kernel(x_ref, mod_ref, g_ref, wr_ref, hp_ref, lg_ref):
    h = _norm_mod(x_ref[...], g_ref[...], mod_ref[0, 3:4, :], mod_ref[0, 4:5, :])
    lg_ref[...] = _dot(h, wr_ref[...], precision=lax.Precision.HIGHEST)
    _store_rows(hp_ref, _pack_rows(h))


def _ffn_in(x1, mod3, g, w_router_pad, T):
    N, D = x1.shape
    assert D // 2 == ROW_TILE * LANES
    tm = 512
    per_b = T // tm
    return pl.pallas_call(
        _ffn_in_kernel,
        grid=(N // tm,),
        in_specs=[pl.BlockSpec((tm, D), lambda i: (i, 0)),
                  pl.BlockSpec((1, 6, D), lambda i: (i // per_b, 0, 0)),
                  pl.BlockSpec((1, D), lambda i: (0, 0)),
                  pl.BlockSpec((D, LANES), lambda i: (0, 0))],
        out_specs=[pl.BlockSpec((tm * ROW_TILE, LANES), lambda i: (i, 0)),
                   pl.BlockSpec((tm, LANES), lambda i: (i, 0))],
        out_shape=[jax.ShapeDtypeStruct((N * ROW_TILE, LANES), U32),
                   jax.ShapeDtypeStruct((N, LANES), F32)],
        compiler_params=_cparams("arbitrary"),
        name="ffn_in",
    )(x1, mod3, g.reshape(1, D), w_router_pad)


def _route_kernel(lg_ref, b_ref, tri_ref, e_ref, r_ref, w_ref, cnt_ref, run_scr):
    tm = lg_ref.shape[0]
    E, GS, NG = N_EXPERTS, GROUP_SIZE, N_GROUPS

    @pl.when(pl.program_id(0) == 0)
    def _():
        run_scr[...] = jnp.zeros_like(run_scr)

    s = jax.nn.sigmoid(jnp.transpose(lg_ref[...])[:E, :])
    sb = s + b_ref[:, 0:1]
    gi = lax.broadcasted_iota(I32, (NG, tm), 0)
    gs = jnp.zeros((NG, tm), F32)
    for j in range(NG):
        blk = sb[j * GS:(j + 1) * GS, :]
        m1 = jnp.max(blk, axis=0, keepdims=True)
        first = jnp.min(jnp.where(blk == m1, gi, GS), axis=0, keepdims=True)
        m2 = jnp.max(jnp.where(gi == first, -jnp.inf, blk), axis=0, keepdims=True)
        gs = jnp.where(gi == j, m1 + m2, gs)

    grank = jnp.zeros((NG, tm), I32)
    for j in range(NG):
        rowv = gs[j:j + 1, :]
        grank = grank + ((rowv > gs) | ((rowv == gs) & (gi > j))).astype(I32)
    gsel = jnp.where(grank < TOP_GROUPS, 1.0, 0.0)

    ei = lax.broadcasted_iota(I32, (E, tm), 0)
    esel = jnp.zeros((E, tm), F32)
    for j in range(NG):
        esel = jnp.where(ei // GS == j, gsel[j:j + 1, :], esel)
    masked = jnp.where(esel > 0.5, sb, NEG)

    erank = jnp.zeros((E, tm), I32)
    for j in range(E):
        rowv = masked[j:j + 1, :]
        erank = erank + ((rowv > masked) | ((rowv == masked) & (ei > j))).astype(I32)
    selm = erank < TOP_K

    wsum = jnp.sum(jnp.where(selm, s, 0.0), axis=0, keepdims=True)
    wfull = jnp.where(selm, s / wsum * ROUTE_SCALE, 0.0)

    self32 = selm.astype(F32)
    before = _dot(self32.astype(BF16), tri_ref[...]) + run_scr[:, 0:1]
    run_new = run_scr[:, 0:1] + jnp.sum(self32, axis=1, keepdims=True)
    run_scr[...] = jnp.broadcast_to(run_new, run_scr.shape)
    cnt_ref[...] = jnp.broadcast_to(run_new, cnt_ref.shape).astype(I32)

    before_i = before.astype(I32)
    for k in range(TOP_K):
        hit = selm & (erank == k)
        e_ref[k:k + 1, :] = jnp.sum(jnp.where(hit, ei, 0), axis=0, keepdims=True)
        r_ref[k:k + 1, :] = jnp.sum(jnp.where(hit, before_i, 0), axis=0, keepdims=True)
        w_ref[k:k + 1, :] = jnp.sum(jnp.where(hit, wfull, 0.0), axis=0, keepdims=True)


def _route(logits, b_router):
    N = logits.shape[0]
    tm = 256
    tri = jnp.asarray(np.triu(np.ones((tm, tm), np.float32), 1), BF16)
    b_col = jnp.broadcast_to(b_router.reshape(N_EXPERTS, 1), (N_EXPERTS, LANES))
    kspec = pl.BlockSpec((TOP_K, tm), lambda i: (0, i))
    return pl.pallas_call(
        _route_kernel,
        grid=(N // tm,),
        in_specs=[pl.BlockSpec((tm, LANES), lambda i: (i, 0)),
                  pl.BlockSpec((N_EXPERTS, LANES), lambda i: (0, 0)),
                  pl.BlockSpec((tm, tm), lambda i: (0, 0))],
        out_specs=[kspec, kspec, kspec, pl.BlockSpec((N_EXPERTS, LANES), lambda i: (0, 0))],
        out_shape=[jax.ShapeDtypeStruct((TOP_K, N), I32), jax.ShapeDtypeStruct((TOP_K, N), I32),
                   jax.ShapeDtypeStruct((TOP_K, N), F32), jax.ShapeDtypeStruct((N_EXPERTS, LANES), I32)],
        scratch_shapes=[pltpu.VMEM((N_EXPERTS, LANES), F32)],
        compiler_params=_cparams("arbitrary"),
        name="moe_route",
    )(logits, b_col, tri)


def _dispatch_kernel(dst_ref, ps_ref, cnt_ref, pe_ref, h_ref, wgu_ref, wd_ref, weg_ref, weu_ref,
                     xs_ref, sh_ref, wegb_ref, weub_ref, zero_scr, sem, zsem):
    i = pl.program_id(0)
    tm = h_ref.shape[0] // ROW_TILE

    def tile_at(ref, r):
        return ref.at[pl.ds(pl.multiple_of(r * ROW_TILE, ROW_TILE), ROW_TILE), :]

    @pl.when(i == 0)
    def _():
        _store_zero_rows(zero_scr)

        def per_expert(e, n):
            lo = ps_ref[e] + cnt_ref[e]
            hi = pe_ref[e]

            def issue(rw, c):
                pltpu.make_async_copy(tile_at(zero_scr, 0), tile_at(xs_ref, rw), zsem).start()
                return c
            lax.fori_loop(lo, hi, issue, 0)
            return n + (hi - lo)
        n_pad = lax.fori_loop(0, N_EXPERTS, per_expert, 0)

        def drain(_, c):
            pltpu.make_async_copy(tile_at(zero_scr, 0), tile_at(xs_ref, 0), zsem).wait()
            return c
        lax.fori_loop(0, n_pad, drain, 0)

        blk_rows = zero_scr.shape[0]

        def tail_copy(b):
            return pltpu.make_async_copy(
                zero_scr, xs_ref.at[pl.ds(pl.multiple_of(b * blk_rows, blk_rows), blk_rows), :], zsem)
        first_free = pe_ref[N_EXPERTS - 1] * ROW_TILE // blk_rows
        n_blocks = xs_ref.shape[0] // blk_rows
        lax.fori_loop(first_free, n_blocks, lambda b, c: (tail_copy(b).start(), c)[1], 0)
        lax.fori_loop(first_free, n_blocks, lambda b, c: (tail_copy(b).wait(), c)[1], 0)

    def issue_row(r, c):
        src = tile_at(h_ref, r)
        for k in range(TOP_K):
            pltpu.make_async_copy(src, tile_at(xs_ref, dst_ref[k, r]), sem).start()
        return c
    lax.fori_loop(0, tm, issue_row, 0)

    wegb_ref[...] = weg_ref[...].astype(BF16)
    weub_ref[...] = weu_ref[...].astype(BF16)
    words = jnp.concatenate([_load_row_chunk(h_ref, c, tm) for c in range(ROW_TILE)], axis=1)
    lo, hi = _unpack_rows(words)
    lo, hi = lo.astype(BF16), hi.astype(BF16)
    half = lo.shape[1]
    gu = _dot(lo, wgu_ref[:half, :]) + _dot(hi, wgu_ref[half:, :])
    ff = gu.shape[1] // 2
    gate, up = gu[:, :ff], gu[:, ff:]
    act = (gate * jax.nn.sigmoid(gate) * up).astype(BF16)
    sh_ref[...] = _dot(act, wd_ref[...]).astype(BF16)

    for k in range(TOP_K):
        pltpu.make_async_copy(h_ref, xs_ref.at[pl.ds(0, tm * ROW_TILE), :], sem).wait()


def _dispatch(dest, pad_start, counts, pad_end, h_packed, w_sh_gu, w_sh_down, w_e_gate, w_e_up, n_rows):
    N = h_packed.shape[0] // ROW_TILE
    D = w_sh_gu.shape[0]
    E, _, FF = w_e_gate.shape
    tm = N // E
    assert tm % LANES == 0
    wspec = lambda: pl.BlockSpec((1, D, FF), lambda i: (i, 0, 0))
    whole = lambda: pl.BlockSpec(memory_space=pltpu.SMEM)
    return pl.pallas_call(
        _dispatch_kernel,
        grid=(N // tm,),
        in_specs=[pl.BlockSpec((TOP_K, tm), lambda i: (0, i), memory_space=pltpu.SMEM),
                  whole(), whole(), whole(),
                  pl.BlockSpec((tm * ROW_TILE, LANES), lambda i: (i, 0)),
                  pl.BlockSpec(w_sh_gu.shape, lambda i: (0, 0)),
                  pl.BlockSpec(w_sh_down.shape, lambda i: (0, 0)),
                  wspec(), wspec()],
        out_specs=[pl.BlockSpec(memory_space=pl.ANY), pl.BlockSpec((tm, D), lambda i: (i, 0)),
                   wspec(), wspec()],
        out_shape=[jax.ShapeDtypeStruct((n_rows * ROW_TILE, LANES), U32),
                   jax.ShapeDtypeStruct((N, D), BF16),
                   jax.ShapeDtypeStruct(w_e_gate.shape, BF16),
                   jax.ShapeDtypeStruct(w_e_up.shape, BF16)],
        scratch_shapes=[pltpu.VMEM((MOE_BLK * ROW_TILE, LANES), U32),
                        pltpu.SemaphoreType.DMA, pltpu.SemaphoreType.DMA],
        compiler_params=_cparams("arbitrary"),
        name="moe_dispatch",
    )(dest, pad_start, counts, pad_end, h_packed, w_sh_gu, w_sh_down, w_e_gate, w_e_up)


def _expert_kernel(bexp_ref, nused_ref, xs_ref, wg_ref, wu_ref, wd_ref, ys_ref, wd_scr):
    i = pl.program_id(0)
    used = i < nused_ref[0]

    @pl.when(jnp.logical_not(used))
    def _():
        _store_zero_rows(ys_ref)

    @pl.when(used & ((i == 0) | (bexp_ref[i] != bexp_ref[jnp.maximum(i - 1, 0)])))
    def _():
        wd_scr[...] = wd_ref[0].astype(BF16)

    @pl.when(used)
    def _():
        m = xs_ref.shape[0] // ROW_TILE
        words = jnp.concatenate([_load_row_chunk(xs_ref, c, m) for c in range(ROW_TILE)], axis=1)
        lo, hi = _unpack_rows(words)
        lo, hi = lo.astype(BF16), hi.astype(BF16)
        half = lo.shape[1]
        gate = _dot(lo, wg_ref[0, :half, :]) + _dot(hi, wg_ref[0, half:, :])
        up = _dot(lo, wu_ref[0, :half, :]) + _dot(hi, wu_ref[0, half:, :])
        act = (gate * jax.nn.sigmoid(gate) * up).astype(BF16)
        _store_rows(ys_ref, _pack_rows(_dot(act, wd_scr[...])))


def _experts(blk_exp, n_used, xs, w_gate, w_up, w_down):
    P = xs.shape[0] // ROW_TILE
    n_blk = P // MOE_BLK
    D, FF = w_gate.shape[1], w_gate.shape[2]
    row_map = lambda i, be, nu: (jnp.minimum(i, nu[0] - 1), 0)
    return pl.pallas_call(
        _expert_kernel,
        grid_spec=pltpu.PrefetchScalarGridSpec(
            num_scalar_prefetch=2,
            grid=(n_blk,),
            in_specs=[pl.BlockSpec((MOE_BLK * ROW_TILE, LANES), row_map),
                      pl.BlockSpec((1, D, FF), lambda i, be, nu: (be[i], 0, 0)),
                      pl.BlockSpec((1, D, FF), lambda i, be, nu: (be[i], 0, 0)),
                      pl.BlockSpec((1, FF, D), lambda i, be, nu: (be[i], 0, 0))],
            out_specs=pl.BlockSpec((MOE_BLK * ROW_TILE, LANES), lambda i, be, nu: (i, 0)),
            scratch_shapes=[pltpu.VMEM((FF, D), BF16)]),
        out_shape=jax.ShapeDtypeStruct(xs.shape, U32),
        compiler_params=_cparams("arbitrary"),
        name="moe_experts",
    )(blk_exp, n_used, xs, w_gate, w_up, w_down)


def _combine_kernel(dcur_ref, dnxt_ref, w_ref, sh_ref, x_ref, mod_ref, g_ref, ys_ref, o_ref, buf2, sems):
    i = pl.program_id(0)
    n = pl.num_programs(0)
    tm = x_ref.shape[0]
    half = ROW_TILE * LANES

    def tile_at(ref, r):
        return ref.at[pl.ds(pl.multiple_of(r * ROW_TILE, ROW_TILE), ROW_TILE), :]

    def gather(slots_ref, slot):
        def issue_row(r, c):
            for k in range(TOP_K):
                pltpu.make_async_copy(tile_at(ys_ref, slots_ref[k, r]),
                                      tile_at(buf2.at[slot, k], r), sems.at[slot]).start()
            return c
        lax.fori_loop(0, tm, issue_row, 0)

    @pl.when(i == 0)
    def _():
        gather(dcur_ref, 0)

    @pl.when(i + 1 < n)
    def _():
        gather(dnxt_ref, (i + 1) % 2)

    slot = i % 2
    buf = buf2.at[slot]
    for k in range(TOP_K):
        pltpu.make_async_copy(ys_ref.at[pl.ds(0, tm * ROW_TILE), :], buf.at[k], sems.at[slot]).wait()

    w = w_ref[...]
    ssq = jnp.zeros((tm, 1), F32)
    for c in range(ROW_TILE):
        c_lo = slice(c * LANES, (c + 1) * LANES)
        c_hi = slice(half + c * LANES, half + (c + 1) * LANES)
        acc_lo = sh_ref[:, c_lo].astype(F32)
        acc_hi = sh_ref[:, c_hi].astype(F32)
        for k in range(TOP_K):
            lo, hi = _unpack_rows(_load_row_chunk(buf.at[k], c, tm))
            wk = w[:, k:k + 1]
            acc_lo = acc_lo + wk * lo
            acc_hi = acc_hi + wk * hi
        o_ref[:, c_lo] = acc_lo
        o_ref[:, c_hi] = acc_hi
        ssq = ssq + jnp.sum(acc_lo * acc_lo, axis=-1, keepdims=True) + jnp.sum(acc_hi * acc_hi, axis=-1, keepdims=True)
    inv = lax.rsqrt(ssq / (2 * half) + EPS)
    o_ref[...] = x_ref[...] + mod_ref[0, 5:6, :] * (o_ref[...] * inv * g_ref[...])


def _combine(dest, w_tok, shared, x1, mod3, g, ys, T):
    N, D = x1.shape
    tm = 256
    per_b = T // tm
    last = N // tm - 1
    return pl.pallas_call(
        _combine_kernel,
        grid=(N // tm,),
        in_specs=[pl.BlockSpec((TOP_K, tm), lambda i: (0, i), memory_space=pltpu.SMEM),
                  pl.BlockSpec((TOP_K, tm), lambda i: (0, jnp.minimum(i + 1, last)), memory_space=pltpu.SMEM),
                  pl.BlockSpec((tm, TOP_K), lambda i: (i, 0)),
                  pl.BlockSpec((tm, D), lambda i: (i, 0)),
                  pl.BlockSpec((tm, D), lambda i: (i, 0)),
                  pl.BlockSpec((1, 6, D), lambda i: (i // per_b, 0, 0)),
                  pl.BlockSpec((1, D), lambda i: (0, 0)),
                  pl.BlockSpec(memory_space=pl.ANY)],
        out_specs=pl.BlockSpec((tm, D), lambda i: (i, 0)),
        out_shape=jax.ShapeDtypeStruct((N, D), F32),
        scratch_shapes=[pltpu.VMEM((2, TOP_K, tm * ROW_TILE, LANES), U32), pltpu.SemaphoreType.DMA((2,))],
        compiler_params=_cparams("arbitrary"),
        name="moe_combine",
    )(dest, dest, w_tok, shared, x1, mod3, g.reshape(1, D), ys)


def _inproj_columns():
    starts = np.concatenate([[0], np.cumsum(IN_SIZES)])
    s_q, s_kv, s_ga, s_qkm, s_vm, s_if, s_om, s_a, s_b = starts[:9]
    cols = np.full((PROJ_W,), D_IN, np.int64)
    cols[OFF_Q:OFF_Q + NSA_Q] = s_q + np.arange(NSA_Q)
    cols[OFF_GATE_A:OFF_GATE_A + D_MODEL] = s_a + np.arange(D_MODEL)
    cols[OFF_GATE_B:OFF_GATE_B + D_MODEL] = s_b + np.arange(D_MODEL)
    for br in range(3):
        for g in range(NSA_KV_GROUPS):
            for kv in range(2):
                dst = OFF_KV + (br * NSA_KV_GROUPS + g) * LANES + kv * NSA_DH
                src = s_kv + (2 * br + kv) * NSA_KV + g * NSA_DH
                cols[dst:dst + NSA_DH] = src + np.arange(NSA_DH)
    cols[OFF_OM:OFF_OM + M_V] = s_om + np.arange(M_V)
    cols[OFF_VM:OFF_VM + M_V] = s_vm + np.arange(M_V)
    cols[OFF_QKM:OFF_QKM + 2 * M_QK] = s_qkm + np.arange(2 * M_QK)
    per_g = 3 * NSA_HPG
    for g in range(NSA_KV_GROUPS):
        cols[OFF_GA + g * LANES:OFF_GA + g * LANES + per_g] = s_ga + g * per_g + np.arange(per_g)
    cols_if = np.full((LANES,), D_IN, np.int64)
    cols_if[:2 * M_HEADS] = s_if + np.arange(2 * M_HEADS)
    return cols, cols_if


def _take_columns(w, cols):
    pieces = []
    start = 0
    for i in range(1, len(cols) + 1):
        run_ends = (i == len(cols) or (cols[start] == D_IN) != (cols[i] == D_IN)
                    or (cols[start] != D_IN and cols[i] != cols[i - 1] + 1))
        if run_ends:
            if cols[start] == D_IN:
                pieces.append(jnp.zeros((w.shape[0], i - start), w.dtype))
            else:
                pieces.append(w[:, int(cols[start]):int(cols[start]) + i - start])
            start = i
    return jnp.concatenate(pieces, axis=1)


def _cmp_to_sel(n_cmp, n_sel):
    ci = np.arange(n_cmp)[:, None]
    sj = np.arange(n_sel)[None, :]
    overlap = (np.minimum(ci * CMP_STRIDE + CMP_LEN, (sj + 1) * SEL_LEN)
               - np.maximum(ci * CMP_STRIDE, sj * SEL_LEN))
    m = np.zeros((LANES, LANES), np.float32)
    m[:n_cmp, :n_sel] = np.clip(overlap, 0, None).astype(np.float32) / CMP_LEN
    return m


def _layer(x, mod3, g_pre_mix, g_post_mix, w_in, cmp_pe, cmp_w1, cmp_w2, conv_w, conv_b, b_gates_m,
           mh_norm_g, w_up_nsa, w_up_mlstm, w_out, g_pre_ffn, g_post_ffn, w_router, b_router,
           w_e_gate, w_e_up, w_e_down, w_sh_gate, w_sh_up, w_sh_down):
    B, T, D = x.shape
    N = B * T
    x2 = x.reshape(N, D)

    cols, cols_if = _inproj_columns()
    w_in_b = w_in.astype(BF16)
    proj, gates = _inproj(x2, mod3, g_pre_mix, _take_columns(w_in_b, cols), _take_columns(w_in_b, cols_if), T)
    proj3 = proj.reshape(B, T, PROJ_W)

    nb = T // CMP_STRIDE
    kvc = _compress(proj3, cmp_pe, cmp_w1, cmp_w2)

    n_sel = T // SEL_LEN
    o_nsa = _nsa_attention(proj3, kvc, jnp.asarray(_cmp_to_sel(nb - 1, n_sel).T))

    L = min(MLSTM_CHUNK, T)
    gates3 = gates.reshape(B, T, LANES)
    gates_row = gates3[:, :, :2 * M_HEADS].reshape(B, T // L, L, 2 * M_HEADS).transpose(0, 1, 3, 2)
    o_ml = _mlstm(proj3, gates3, gates_row, conv_w, conv_b, b_gates_m, mh_norm_g)

    merged = _merge(o_nsa.reshape(N, NSA_Q), o_ml.reshape(N, M_V), proj,
                    w_up_nsa.astype(BF16), w_up_mlstm.astype(BF16))
    x1 = _outproj(merged, w_out.astype(BF16), x2, mod3, g_post_mix, T)

    w_router_pad = jnp.pad(w_router, ((0, 0), (0, LANES - N_EXPERTS)))
    w_sh_gu = jnp.concatenate([w_sh_gate, w_sh_up], axis=1).astype(BF16)
    h_packed, logits = _ffn_in(x1, mod3, g_pre_ffn, w_router_pad, T)
    eidx, rnk, w_k, counts2 = _route(logits, b_router)

    counts = counts2[:, 0]
    padded = (counts + MOE_BLK - 1) // MOE_BLK * MOE_BLK
    pad_end = jnp.cumsum(padded).astype(I32)
    pad_start = pad_end - padded
    n_blk = (N * TOP_K) // MOE_BLK + N_EXPERTS
    blk_first = jnp.arange(n_blk, dtype=I32) * MOE_BLK
    blk_exp = jnp.minimum(jnp.sum((pad_end[None, :] <= blk_first[:, None]).astype(I32), axis=1), N_EXPERTS - 1)
    n_used = (pad_end[-1:] // MOE_BLK).astype(I32)
    experts = jnp.arange(N_EXPERTS, dtype=I32)
    dest = rnk + jnp.sum(jnp.where(eidx[:, :, None] == experts, pad_start, 0), axis=-1)

    xs, shared, w_gate_b, w_up_b = _dispatch(dest, pad_start, counts, pad_end, h_packed, w_sh_gu,
                                             w_sh_down.astype(BF16), w_e_gate, w_e_up, n_blk * MOE_BLK)
    ys = _experts(blk_exp, n_used, xs, w_gate_b, w_up_b, w_e_down)
    out = _combine(dest, jnp.transpose(w_k), shared, x1, mod3, g_post_ffn, ys, T)
    return out.reshape(B, T, D)


def kernel(x, c, w_ada, b_ada, g_pre_mix, g_post_mix, w_in, cmp_pe, cmp_w1, cmp_w2, conv_w, conv_b, b_gates_m, mh_norm_g, w_up_nsa, w_up_mlstm, w_out, g_pre_ffn, g_post_ffn, w_router, b_router, w_e_gate, w_e_up, w_e_down, w_sh_gate, w_sh_up, w_sh_down):
    B = x.shape[0]
    for l in range(w_ada.shape[0]):
        mod3 = _ada(c, w_ada[l], b_ada[l]).reshape(B, 6, D_MODEL)
        x = _layer(x, mod3, g_pre_mix[l], g_post_mix[l], w_in[l], cmp_pe[l], cmp_w1[l], cmp_w2[l],
                   conv_w[l], conv_b[l], b_gates_m[l], mh_norm_g[l], w_up_nsa[l], w_up_mlstm[l], w_out[l],
                   g_pre_ffn[l], g_post_ffn[l], w_router[l], b_router[l], w_e_gate[l], w_e_up[l],
                   w_e_down[l], w_sh_gate[l], w_sh_up[l], w_sh_down[l])
    return x
```
